```python
import jax, jax.numpy as jnp
from jax import lax
import numpy as np

D_MODEL = 1024
BATCH = 4
SEQ = 4096
DEPTH = 4

CHUNK = 64
N_MIXERS = 2
HEAD_SIZE = 64
N_HEADS = D_MODEL // HEAD_SIZE
D_FF = 4 * D_MODEL
CONV_WIDTH = 3
D_DECAY_LORA = 64
D_AAA_LORA = 64
D_MV_LORA = 32
D_GATE_LORA = 160
N_CONV_LAYERS = (DEPTH + 1) // 2
N_RWKV_LAYERS = DEPTH // 2
N_VRES = max(N_RWKV_LAYERS - 1, 0)
N_MOD = 6
NORM_EPS = 1e-6
GN_EPS = 64e-5

kernel_name = "hybrid_shortconv_rwkv7_adaln_encoder"


def _rmsnorm(x, g):
    x32 = x.astype(jnp.float32)
    y = x32 * lax.rsqrt(jnp.mean(x32 * x32, axis=-1, keepdims=True) + NORM_EPS)
    return (y * g.astype(jnp.float32)).astype(x.dtype)


def _modulate(h, shift, scale):
    return h * (1 + scale[:, None, :]) + shift[:, None, :]


def _short_conv_mixer(h, w_in, conv_w, w_out):
    bch = h @ w_in
    b_gate, c_gate, hv = jnp.split(bch, 3, axis=-1)
    u = c_gate * hv
    conv = lax.conv_general_dilated(
        u, conv_w[:, None, :], window_strides=(1,),
        padding=[(CONV_WIDTH - 1, 0)],
        dimension_numbers=('NWC', 'WIO', 'NWC'),
        feature_group_count=D_MODEL)
    return (b_gate * conv) @ w_out


def _wkv7_scan(r, w, k, v, a, b):
    bsz, seq = r.shape[0], r.shape[1]

    def to_chunks(t):
        return t.transpose(1, 0, 2, 3).reshape(seq // CHUNK, CHUNK, bsz, N_HEADS, HEAD_SIZE)

    def frame_step(state, inp):
        r_t, w_t, k_t, v_t, a_t, b_t = inp
        sa = jnp.einsum('bhvk,bhk->bhv', state, a_t)
        state = (state * w_t[:, :, None, :]
                 + sa[..., None] * b_t[:, :, None, :]
                 + v_t[..., None] * k_t[:, :, None, :])
        y_t = jnp.einsum('bhvk,bhk->bhv', state, r_t)
        return state, y_t

    def chunk_step(state, inp_chunk):
        return lax.scan(frame_step, state, inp_chunk)

    s0 = jnp.zeros((bsz, N_HEADS, HEAD_SIZE, HEAD_SIZE), jnp.float32)
    inputs = (to_chunks(r), to_chunks(w), to_chunks(k), to_chunks(v), to_chunks(a), to_chunks(b))
    _, y = lax.scan(chunk_step, s0, inputs)
    return y.reshape(seq, bsz, N_HEADS, HEAD_SIZE).transpose(1, 0, 2, 3)


def _rwkv7_mixer(h, v_first, vres, mu, w_rkv, w_o, w0, w1, w2, a0, a1, a2,
                 g1, g2, k_k, k_a, r_k, ln_w, ln_b):
    bsz, seq, d = h.shape
    h_prev = jnp.pad(h, ((0, 0), (1, 0), (0, 0)))[:, :-1]
    xx = h_prev - h
    xr, xw, xk, xv, xa, xg = h[None] + xx[None] * mu[:, None, None, :]
    r, k, v = jnp.einsum('nbsd,nde->nbse', jnp.stack([xr, xk, xv]), w_rkv)
    w = -jax.nn.softplus(-(w0 + jnp.tanh(xw @ w1) @ w2)) - 0.5
    a = jax.nn.sigmoid(a0 + (xa @ a1) @ a2)
    g = jax.nn.sigmoid(xg @ g1) @ g2
    if vres is None:
        v_first = v
    else:
        v0, v1, v2 = vres
        v = v + (v_first - v) * jax.nn.sigmoid(v0 + (xv @ v1) @ v2)

    def heads(t):
        return t.reshape(bsz, seq, N_HEADS, HEAD_SIZE).astype(jnp.float32)

    kk = heads(k * k_k)
    kk = kk / jnp.maximum(jnp.sqrt(jnp.sum(kk * kk, axis=-1, keepdims=True)), 1e-12)
    k = k * (1 + (a - 1) * k_a)
    rh, kh, vh, ah = heads(r), heads(k), heads(v), heads(a)
    decay = jnp.exp(-jnp.exp(heads(w)))
    y = _wkv7_scan(rh, decay, kh, vh, -kk, kk * ah)
    mean = jnp.mean(y, axis=-1, keepdims=True)
    var = jnp.mean(jnp.square(y - mean), axis=-1, keepdims=True)
    y = (y - mean) * lax.rsqrt(var + GN_EPS)
    y = y * ln_w.reshape(N_HEADS, HEAD_SIZE) + ln_b.reshape(N_HEADS, HEAD_SIZE)
    bonus = jnp.sum(rh * kh * r_k, axis=-1, keepdims=True) * vh
    out = ((y + bonus).reshape(bsz, seq, d).astype(h.dtype) * g) @ w_o
    return out, v_first


def setup_inputs(seed: int = 0) -> dict:
    key = jax.random.key(seed)
    ks = iter(jax.random.split(key, 40))
    D = D_MODEL
    nrm = lambda shape, s: jax.random.normal(next(ks), shape, jnp.float32) * s
    uni = lambda shape, lo, hi: jax.random.uniform(next(ks), shape, jnp.float32, lo, hi)
    NC, NR, NV = N_CONV_LAYERS, N_RWKV_LAYERS, N_VRES
    return {
        "x": nrm((BATCH, SEQ, D), 1.0),
        "c": nrm((BATCH, D), 1.0),
        "norm_g": 1.0 + nrm((DEPTH, 2, D), 0.05),
        "final_g": 1.0 + nrm((D,), 0.05),
        "ada_w": nrm((DEPTH, D, N_MOD * D), 0.5 * D ** -0.5),
        "ada_b": nrm((DEPTH, N_MOD * D), 0.02),
        "conv_w_in": nrm((NC, D, 3 * D), D ** -0.5),
        "conv_w": nrm((NC, CONV_WIDTH, D), CONV_WIDTH ** -0.5),
        "conv_w_out": nrm((NC, D, D), D ** -0.5),
        "rw_mu": uni((NR, 6, D), 0.0, 1.0),
        "rw_w_rkv": nrm((NR, 3, D, D), D ** -0.5),
        "rw_w_o": nrm((NR, D, D), D ** -0.5),
        "rw_w0": uni((NR, D), -6.5, -1.5),
        "rw_w1": nrm((NR, D, D_DECAY_LORA), D ** -0.5),
        "rw_w2": nrm((NR, D_DECAY_LORA, D), 0.1 * D_DECAY_LORA ** -0.5),
        "rw_a0": nrm((NR, D), 0.1),
        "rw_a1": nrm((NR, D, D_AAA_LORA), D ** -0.5),
        "rw_a2": nrm((NR, D_AAA_LORA, D), 0.1 * D_AAA_LORA ** -0.5),
        "rw_g1": nrm((NR, D, D_GATE_LORA), D ** -0.5),
        "rw_g2": nrm((NR, D_GATE_LORA, D), D_GATE_LORA ** -0.5),
        "rw_k_k": 0.85 + nrm((NR, D), 0.05),
        "rw_k_a": 1.0 + nrm((NR, D), 0.05),
        "rw_r_k": nrm((NR, N_HEADS, HEAD_SIZE), 0.1),
        "rw_ln_w": 1.0 + nrm((NR, D), 0.05),
        "rw_ln_b": nrm((NR, D), 0.02),
        "rw_v0": 1.0 + nrm((NV, D), 0.1),
        "rw_v1": nrm((NV, D, D_MV_LORA), D ** -0.5),
        "rw_v2": nrm((NV, D_MV_LORA, D), 0.1 * D_MV_LORA ** -0.5),
        "mlp_w1": nrm((DEPTH, D, D_FF), D ** -0.5),
        "mlp_w2": nrm((DEPTH, D_FF, D), D_FF ** -0.5),
    }


def reference(x, c, norm_g, final_g, ada_w, ada_b, conv_w_in, conv_w, conv_w_out,
              rw_mu, rw_w_rkv, rw_w_o, rw_w0, rw_w1, rw_w2, rw_a0, rw_a1, rw_a2,
              rw_g1, rw_g2, rw_k_k, rw_k_a, rw_r_k, rw_ln_w, rw_ln_b,
              rw_v0, rw_v1, rw_v2, mlp_w1, mlp_w2):
    c_act = jax.nn.silu(c)
    v_first = None
    for i in range(DEPTH):
        mod = c_act @ ada_w[i] + ada_b[i]
        sh1, sc1, gt1, sh2, sc2, gt2 = jnp.split(mod, N_MOD, axis=-1)
        h = _modulate(_rmsnorm(x, norm_g[i, 0]), sh1, sc1)
        j = i // N_MIXERS
        if i % N_MIXERS == 0:
            y = _short_conv_mixer(h, conv_w_in[j], conv_w[j], conv_w_out[j])
        else:
            vres = None if v_first is None else (rw_v0[j - 1], rw_v1[j - 1], rw_v2[j - 1])
            y, v_first = _rwkv7_mixer(
                h, v_first, vres, rw_mu[j], rw_w_rkv[j], rw_w_o[j], rw_w0[j], rw_w1[j],
                rw_w2[j], rw_a0[j], rw_a1[j], rw_a2[j], rw_g1[j], rw_g2[j], rw_k_k[j],
                rw_k_a[j], rw_r_k[j], rw_ln_w[j], rw_ln_b[j])
        x = x + gt1[:, None, :] * y
        h = _modulate(_rmsnorm(x, norm_g[i, 1]), sh2, sc2)
        x = x + gt2[:, None, :] * (jnp.square(jax.nn.relu(h @ mlp_w1[i])) @ mlp_w2[i])
    return _rmsnorm(x, final_g)
```

```python
import functools

import jax
import jax.numpy as jnp
from jax import lax
from jax.experimental import pallas as pl
from jax.experimental.pallas import tpu as pltpu

F32 = jnp.float32
BF16 = jnp.bfloat16

HEAD_SIZE = 64
WKV_CHUNK = 64
GROUP_LANES = 256
NORM_EPS = 1e-6
GN_EPS = 64e-5
VMEM_LIMIT_BYTES = 56 * 1024 * 1024

MLP_TM = 1024
MLP_TF = 1024
CONV_TM = 512
RWKV_IN_TM = 256
WKV_TT = 256
RWKV_OUT_TM = 512


def _params(*semantics):
    return pltpu.CompilerParams(dimension_semantics=semantics,
                                vmem_limit_bytes=VMEM_LIMIT_BYTES)


def _dot(a, b):
    return jnp.dot(a, b, preferred_element_type=F32)


def _dot_nt(a, b):
    return lax.dot_general(a, b, (((1,), (1,)), ((), ())), preferred_element_type=F32)


def _dot_tn(a, b):
    return lax.dot_general(a, b, (((0,), (0,)), ((), ())), preferred_element_type=F32)


def _sigmoid(x):
    return 1.0 / (1.0 + jnp.exp(-x))


def _norm_mod(x, a, s):
    ms = jnp.mean(x * x, axis=-1, keepdims=True)
    return x * lax.rsqrt(ms + NORM_EPS) * a + s


def _split2(x):
    hi = x.astype(BF16)
    lo = (x - hi.astype(F32)).astype(BF16)
    return hi, lo


def _head_sum(x, e):
    hi, lo = _split2(x)
    return _dot(hi, e) + _dot(lo, e)


def _head_bcast(s, et):
    hi, lo = _split2(s)
    return _dot(hi, et) + _dot(lo, et)


def _head_indicator(d):
    h = d // HEAD_SIZE
    e = (lax.broadcasted_iota(jnp.int32, (d, h), 0) // HEAD_SIZE
         == lax.broadcasted_iota(jnp.int32, (d, h), 1))
    et = (lax.broadcasted_iota(jnp.int32, (h, d), 1) // HEAD_SIZE
          == lax.broadcasted_iota(jnp.int32, (h, d), 0))
    return (jnp.where(e, 1.0, 0.0).astype(BF16), jnp.where(et, 1.0, 0.0).astype(BF16))


def _ada_kernel(c_ref, w_ref, b_ref, o_ref):
    c = c_ref[...]
    ca = (c * _sigmoid(c)).astype(BF16)
    o_ref[0] = _dot(ca, w_ref[0].astype(BF16)) + b_ref[0]


def _ada_mod(c, ada_w, ada_b):
    depth, d, n = ada_w.shape
    bsz = c.shape[0]
    rows = 16
    cp = jnp.zeros((rows, d), F32).at[:bsz].set(c)
    tn = min(n, 2048)
    out = pl.pallas_call(
        _ada_kernel,
        out_shape=jax.ShapeDtypeStruct((depth, rows, n), F32),
        grid=(depth, n // tn),
        in_specs=[pl.BlockSpec((rows, d), lambda i, j: (0, 0)),
                  pl.BlockSpec((1, d, tn), lambda i, j: (i, 0, j)),
                  pl.BlockSpec((1, 1, tn), lambda i, j: (i, 0, j))],
        out_specs=pl.BlockSpec((1, rows, tn), lambda i, j: (i, 0, j)),
        compiler_params=_params("parallel", "parallel"),
        name="ada_mod",
    )(cp, ada_w, ada_b.reshape(depth, 1, n))
    return out[:, :bsz]


def _mlp_kernel(x_ref, a_ref, s_ref, g_ref, fg_ref, w1_ref, w2_ref, o_ref, h_ref, acc_ref,
                *, final_norm):
    f = pl.program_id(1)

    @pl.when(f == 0)
    def _():
        h_ref[...] = _norm_mod(x_ref[...], a_ref[0], s_ref[0]).astype(BF16)
        acc_ref[...] = jnp.zeros_like(acc_ref)

    z = jnp.maximum(_dot(h_ref[...], w1_ref[...]), 0.0)
    acc_ref[...] += _dot((z * z).astype(BF16), w2_ref[...])

    @pl.when(f == pl.num_programs(1) - 1)
    def _():
        y = x_ref[...] + g_ref[0] * acc_ref[...]
        if final_norm:
            ms = jnp.mean(y * y, axis=-1, keepdims=True)
            y = y * lax.rsqrt(ms + NORM_EPS) * fg_ref[...]
        o_ref[...] = y


def _mlp(x, a, s, g, fg, w1, w2, seq, final_norm):
    m, d = x.shape
    ff = w1.shape[1]
    tm = min(seq, MLP_TM)
    tf = min(ff, MLP_TF)
    per_seq = seq // tm
    vec = pl.BlockSpec((1, 1, d), lambda i, f: (i // per_seq, 0, 0))
    return pl.pallas_call(
        functools.partial(_mlp_kernel, final_norm=final_norm),
        out_shape=jax.ShapeDtypeStruct((m, d), F32),
        grid=(m // tm, ff // tf),
        in_specs=[pl.BlockSpec((tm, d), lambda i, f: (i, 0)), vec, vec, vec,
                  pl.BlockSpec((1, d), lambda i, f: (0, 0)),
                  pl.BlockSpec((d, tf), lambda i, f: (0, f)),
                  pl.BlockSpec((tf, d), lambda i, f: (f, 0))],
        out_specs=pl.BlockSpec((tm, d), lambda i, f: (i, 0)),
        scratch_shapes=[pltpu.VMEM((tm, d), BF16), pltpu.VMEM((tm, d), F32)],
        compiler_params=_params("parallel", "arbitrary"),
        name="mlp",
    )(x, a, s, g, fg, w1, w2)


def _conv_in_kernel(x_ref, a_ref, s_ref, wb_ref, wc_ref, wh_ref, b_ref, u_ref):
    h = _norm_mod(x_ref[...], a_ref[0], s_ref[0]).astype(BF16)
    b_ref[...] = _dot(h, wb_ref[...])
    u_ref[...] = _dot(h, wc_ref[...]) * _dot(h, wh_ref[...])


def _conv_in(x, a, s, wb, wc, wh, seq):
    m, d = x.shape
    tm = min(seq, CONV_TM)
    per_seq = seq // tm
    vec = pl.BlockSpec((1, 1, d), lambda i: (i // per_seq, 0, 0))
    wspec = pl.BlockSpec((d, d), lambda i: (0, 0))
    tile = pl.BlockSpec((tm, d), lambda i: (i, 0))
    return pl.pallas_call(
        _conv_in_kernel,
        out_shape=(jax.ShapeDtypeStruct((m, d), F32), jax.ShapeDtypeStruct((m, d), F32)),
        grid=(m // tm,),
        in_specs=[tile, vec, vec, wspec, wspec, wspec],
        out_specs=(tile, tile),
        compiler_params=_params("parallel"),
        name="conv_in",
    )(x, a, s, wb, wc, wh)


def _conv_out_kernel(u_ref, b_ref, x_ref, cw_ref, g_ref, wo_ref, o_ref, carry_ref, *, per_seq):
    i = pl.program_id(0)
    tm = u_ref.shape[0]

    @pl.when(i % per_seq == 0)
    def _():
        carry_ref[...] = jnp.zeros_like(carry_ref)

    u = u_ref[...]
    prev = carry_ref[...]
    row = lax.broadcasted_iota(jnp.int32, (tm, 1), 0)
    u1 = jnp.where(row == 0, prev[7:8], pltpu.roll(u, 1, 0))
    u2 = jnp.where(row == 0, prev[6:7], jnp.where(row == 1, prev[7:8], pltpu.roll(u, 2, 0)))
    cw = cw_ref[...]
    conv = cw[0:1] * u2 + cw[1:2] * u1 + cw[2:3] * u
    y = (b_ref[...] * conv).astype(BF16)
    o_ref[...] = x_ref[...] + g_ref[0] * _dot(y, wo_ref[...])
    carry_ref[...] = u[tm - 8:tm]


def _conv_out(u, b, x, cw, g, wo, seq):
    m, d = x.shape
    tm = min(seq, CONV_TM)
    per_seq = seq // tm
    tile = pl.BlockSpec((tm, d), lambda i: (i, 0))
    return pl.pallas_call(
        functools.partial(_conv_out_kernel, per_seq=per_seq),
        out_shape=jax.ShapeDtypeStruct((m, d), F32),
        grid=(m // tm,),
        in_specs=[tile, tile, tile,
                  pl.BlockSpec(cw.shape, lambda i: (0, 0)),
                  pl.BlockSpec((1, 1, d), lambda i: (i // per_seq, 0, 0)),
                  pl.BlockSpec((d, d), lambda i: (0, 0))],
        out_specs=tile,
        scratch_shapes=[pltpu.VMEM((8, d), F32)],
        compiler_params=_params("arbitrary"),
        name="conv_out",
    )(u, b, x, cw, g, wo)


def _rwkv_in_kernel(*refs, per_seq, use_vres):
    if use_vres:
        (x_ref, a_ref, s_ref, mu_ref, vec_ref, wr_ref, wk_ref, wv_ref, w1_ref, w2_ref,
         a1_ref, a2_ref, g1_ref, g2_ref, vf_ref, v1_ref, v2_ref,
         r_ref, k_ref, v_ref, lw_ref, kk_ref, kka_ref, g_ref, carry_ref) = refs
    else:
        (x_ref, a_ref, s_ref, mu_ref, vec_ref, wr_ref, wk_ref, wv_ref, w1_ref, w2_ref,
         a1_ref, a2_ref, g1_ref, g2_ref,
         r_ref, k_ref, v_ref, lw_ref, kk_ref, kka_ref, g_ref, carry_ref) = refs
    i = pl.program_id(0)
    tm, d = x_ref.shape

    @pl.when(i % per_seq == 0)
    def _():
        carry_ref[...] = jnp.zeros_like(carry_ref)

    h = _norm_mod(x_ref[...], a_ref[0], s_ref[0])
    prev = carry_ref[...]
    row = lax.broadcasted_iota(jnp.int32, (tm, 1), 0)
    xx = jnp.where(row == 0, prev[7:8], pltpu.roll(h, 1, 0)) - h
    carry_ref[...] = h[tm - 8:tm]

    mu = mu_ref[...]
    vec = vec_ref[...]
    def mix(n):
        return (h + xx * mu[n:n + 1]).astype(BF16)
    xr, xw, xk, xv, xa, xg = (mix(n) for n in range(6))

    r = _dot(xr, wr_ref[...])
    k = _dot(xk, wk_ref[...])
    v = _dot(xv, wv_ref[...])
    wl = vec[0:1] + _dot(jnp.tanh(_dot(xw, w1_ref[...])).astype(BF16), w2_ref[...])
    a = _sigmoid(vec[1:2] + _dot(_dot(xa, a1_ref[...]).astype(BF16), a2_ref[...]))
    g = _dot(_sigmoid(_dot(xg, g1_ref[...])).astype(BF16), g2_ref[...])
    if use_vres:
        mix_v = _sigmoid(vec[4:5] + _dot(_dot(xv, v1_ref[...]).astype(BF16), v2_ref[...]))
        v = v + (vf_ref[...] - v) * mix_v

    lw_ref[...] = -jnp.exp(-0.5) * _sigmoid(wl)

    e, et = _head_indicator(d)
    kk = k * vec[2:3]
    norm = jnp.maximum(jnp.sqrt(_head_sum(kk * kk, e)), 1e-12)
    kk = kk * _head_bcast(1.0 / norm, et)
    r_ref[...] = r
    k_ref[...] = k * (1.0 + (a - 1.0) * vec[3:4])
    v_ref[...] = v
    kk_ref[...] = kk
    kka_ref[...] = kk * a
    g_ref[...] = g


def _rwkv_in(x, a, s, mu, vec, weights, vfirst, vres_w, seq):
    m, d = x.shape
    tm = min(seq, RWKV_IN_TM)
    per_seq = seq // tm
    use_vres = vfirst is not None
    tile = pl.BlockSpec((tm, d), lambda i: (i, 0))
    mvec = pl.BlockSpec((1, 1, d), lambda i: (i // per_seq, 0, 0))

    def full(arr):
        return pl.BlockSpec(arr.shape, lambda i: (0,) * arr.ndim)

    args = [x, a, s, mu, vec] + list(weights)
    specs = [tile, mvec, mvec, full(mu), full(vec)] + [full(w) for w in weights]
    if use_vres:
        args += [vfirst] + list(vres_w)
        specs += [tile] + [full(w) for w in vres_w]
    out = jax.ShapeDtypeStruct((m, d), F32)
    return pl.pallas_call(
        functools.partial(_rwkv_in_kernel, per_seq=per_seq, use_vres=use_vres),
        out_shape=(out,) * 7,
        grid=(m // tm,),
        in_specs=specs,
        out_specs=(tile,) * 7,
        scratch_shapes=[pltpu.VMEM((8, d), F32)],
        compiler_params=_params("arbitrary"),
        name="rwkv_in",
    )(*args)


def _wkv_kernel(r_ref, k_ref, v_ref, lw_ref, kk_ref, kka_ref, rk_ref, lnw_ref, lnb_ref,
                o_ref, state_ref, y_ref, w_s, u0_s, arb_s, ark_s, rt_s, bh_s, kh_s, gl_s):
    tt, d = r_ref.shape
    n_chunks = tt // WKV_CHUNK
    n_groups = d // GROUP_LANES
    c_len, gl = WKV_CHUNK, GROUP_LANES
    reps = gl // c_len

    @pl.when(pl.program_id(1) == 0)
    def _():
        state_ref[...] = jnp.zeros_like(state_ref)

    row = lax.broadcasted_iota(jnp.int32, (c_len, gl), 0)
    col = lax.broadcasted_iota(jnp.int32, (c_len, gl), 1) % c_len
    strict = col < row
    incl = col <= row
    eye = jnp.where(col == row, 1.0, 0.0)
    bd_mask = (lax.broadcasted_iota(jnp.int32, (gl, gl), 0) // c_len
               == lax.broadcasted_iota(jnp.int32, (gl, gl), 1) // c_len)
    tri = jnp.where(lax.broadcasted_iota(jnp.int32, (c_len, c_len), 1)
                    <= lax.broadcasted_iota(jnp.int32, (c_len, c_len), 0), 1.0, 0.0).astype(BF16)

    def bd(x):
        return jnp.where(bd_mask, jnp.concatenate([x] * reps, axis=0), 0.0).astype(BF16)

    def bf(x):
        return x.astype(BF16)

    def prepare(c, _):
        rows = pl.ds(pl.multiple_of(c * c_len, c_len), c_len)
        for gi in range(n_groups):
            lanes = slice(gi * gl, (gi + 1) * gl)
            lw = lw_ref[rows, lanes]
            p0 = lw.astype(BF16)
            rem = lw - p0.astype(F32)
            p1 = rem.astype(BF16)
            p2 = (rem - p1.astype(F32)).astype(BF16)
            cum = _dot(tri, p0) + _dot(tri, p1) + _dot(tri, p2)
            c_last = cum[c_len - 1:c_len]
            kk = kk_ref[rows, lanes]
            kka = kka_ref[rows, lanes]
            kx = k_ref[rows, lanes]
            e_inv = jnp.exp(-cum)
            e_tail = jnp.exp(c_last - cum)
            at = -kk * jnp.exp(cum - lw)
            rt = r_ref[rows, lanes] * jnp.exp(cum)
            lhs = bf(jnp.concatenate([at, rt], axis=0))
            sb = _dot_nt(lhs, bd(kka * e_inv))
            sk = _dot_nt(lhs, bd(kx * e_inv))
            a_ab = jnp.where(strict, sb[:c_len], 0.0)
            a_ak = jnp.where(strict, sk[:c_len], 0.0)
            x = a_ab
            tinv = eye + x
            n_sq = (c_len - 1).bit_length() - 1
            for _ in range(n_sq):
                x = _dot(bf(x), bd(x))
                tinv = tinv + _dot(bf(tinv), bd(x))
            tb = bf(tinv)
            akv = _dot(bf(a_ak), bd(v_ref[rows, lanes]))
            w_s[rows, lanes] = _dot(tb, bd(at))
            u0_s[rows, lanes] = _dot(tb, bd(akv))
            arb_s[rows, lanes] = jnp.where(incl, sb[c_len:], 0.0)
            ark_s[rows, lanes] = jnp.where(incl, sk[c_len:], 0.0)
            rt_s[rows, lanes] = rt
            bh_s[rows, lanes] = kka * e_tail
            kh_s[rows, lanes] = kx * e_tail
            gl_s[pl.ds(pl.multiple_of(c * 8, 8), 8), lanes] = jnp.broadcast_to(
                jnp.exp(c_last), (8, gl))
        return 0

    lax.fori_loop(0, n_chunks, prepare, 0)

    def advance(c, _):
        rows = pl.ds(pl.multiple_of(c * c_len, c_len), c_len)
        for gi in range(n_groups):
            lanes = slice(gi * gl, (gi + 1) * gl)
            sv = state_ref[gi]
            v = v_ref[rows, lanes]
            p = _dot_nt(bf(jnp.concatenate([w_s[rows, lanes], rt_s[rows, lanes]], axis=0)), bf(sv))
            u = u0_s[rows, lanes] + p[:c_len]
            y_ref[rows, lanes] = (p[c_len:] + _dot(bf(arb_s[rows, lanes]), bd(u))
                                  + _dot(bf(ark_s[rows, lanes]), bd(v)))
            upd = _dot_tn(bf(jnp.concatenate([u, v], axis=0)),
                          bf(jnp.concatenate([bh_s[rows, lanes], kh_s[rows, lanes]], axis=0)))
            decay = gl_s[pl.ds(pl.multiple_of(c * 8, 8), 1), lanes]
            state_ref[gi] = sv * decay + jnp.where(bd_mask, upd, 0.0)
        return 0

    lax.fori_loop(0, n_chunks, advance, 0)

    e, et = _head_indicator(d)
    y = y_ref[...]
    inv_n = 1.0 / HEAD_SIZE
    mean = _head_bcast(_head_sum(y, e) * inv_n, et)
    dev = y - mean
    var = _head_sum(dev * dev, e) * inv_n
    yn = dev * _head_bcast(lax.rsqrt(var + GN_EPS), et) * lnw_ref[...] + lnb_ref[...]
    bonus = _head_bcast(_head_sum(r_ref[...] * k_ref[...] * rk_ref[...], e), et) * v_ref[...]
    o_ref[...] = yn + bonus


def _wkv(r, k, v, lw, kk, kka, rk, lnw, lnb, bsz, seq):
    m, d = r.shape
    tt = min(seq, WKV_TT)
    per_seq = seq // tt
    tile = pl.BlockSpec((tt, d), lambda b, t: (b * per_seq + t, 0))
    vec = pl.BlockSpec((1, d), lambda b, t: (0, 0))
    big = pltpu.VMEM((tt, d), F32)
    return pl.pallas_call(
        _wkv_kernel,
        out_shape=jax.ShapeDtypeStruct((m, d), F32),
        grid=(bsz, per_seq),
        in_specs=[tile] * 6 + [vec] * 3,
        out_specs=tile,
        scratch_shapes=[pltpu.VMEM((d // GROUP_LANES, GROUP_LANES, GROUP_LANES), F32),
                        big, big, big, big, big, big, big, big,
                        pltpu.VMEM((tt // WKV_CHUNK * 8, d), F32)],
        compiler_params=_params("parallel", "arbitrary"),
        name="wkv",
    )(r, k, v, lw, kk, kka, rk, lnw, lnb)


def _rwkv_out_kernel(y_ref, g_ref, x_ref, gate_ref, wo_ref, o_ref):
    z = (y_ref[...] * g_ref[...]).astype(BF16)
    o_ref[...] = x_ref[...] + gate_ref[0] * _dot(z, wo_ref[...])


def _rwkv_out(y, g, x, gate, wo, seq):
    m, d = x.shape
    tm = min(seq, RWKV_OUT_TM)
    per_seq = seq // tm
    tile = pl.BlockSpec((tm, d), lambda i: (i, 0))
    return pl.pallas_call(
        _rwkv_out_kernel,
        out_shape=jax.ShapeDtypeStruct((m, d), F32),
        grid=(m // tm,),
        in_specs=[tile, tile, tile,
                  pl.BlockSpec((1, 1, d), lambda i: (i // per_seq, 0, 0)),
                  pl.BlockSpec((d, d), lambda i: (0, 0))],
        out_specs=tile,
        compiler_params=_params("parallel"),
        name="rwkv_out",
    )(y, g, x, gate, wo)


def kernel(x, c, norm_g, final_g, ada_w, ada_b, conv_w_in, conv_w, conv_w_out, rw_mu, rw_w_rkv, rw_w_o, rw_w0, rw_w1, rw_w2, rw_a0, rw_a1, rw_a2, rw_g1, rw_g2, rw_k_k, rw_k_a, rw_r_k, rw_ln_w, rw_ln_b, rw_v0, rw_v1, rw_v2, mlp_w1, mlp_w2):
    bsz, seq, d = x.shape
    depth = ada_w.shape[0]
    assert d % GROUP_LANES == 0 and seq % WKV_CHUNK == 0
    bf = lambda w: w.astype(BF16)

    mod = _ada_mod(c, ada_w, ada_b).reshape(depth, bsz, 6, 1, d)
    fg = final_g.reshape(1, d)
    xf = x.reshape(bsz * seq, d)
    v_first = None
    for i in range(depth):
        sh1, sc1, gt1, sh2, sc2, gt2 = (mod[i, :, n] for n in range(6))
        a1 = norm_g[i, 0] * (1.0 + sc1)
        a2 = norm_g[i, 1] * (1.0 + sc2)
        j = i // 2
        if i % 2 == 0:
            w_in = bf(conv_w_in[j])
            b_gate, u = _conv_in(xf, a1, sh1, w_in[:, :d], w_in[:, d:2 * d], w_in[:, 2 * d:], seq)
            xf = _conv_out(u, b_gate, xf, conv_w[j], gt1, bf(conv_w_out[j]), seq)
        else:
            vec = jnp.stack([rw_w0[j], rw_a0[j], rw_k_k[j], rw_k_a[j],
                             rw_v0[j - 1] if v_first is not None else jnp.zeros((d,), F32)])
            weights = [bf(rw_w_rkv[j, 0]), bf(rw_w_rkv[j, 1]), bf(rw_w_rkv[j, 2]),
                       bf(rw_w1[j]), bf(rw_w2[j]), bf(rw_a1[j]), bf(rw_a2[j]),
                       bf(rw_g1[j]), bf(rw_g2[j])]
            vres_w = None if v_first is None else [bf(rw_v1[j - 1]), bf(rw_v2[j - 1])]
            r, k, v, lw, kk, kka, g = _rwkv_in(xf, a1, sh1, rw_mu[j], vec, weights,
                                               v_first, vres_w, seq)
            if v_first is None:
                v_first = v
            y = _wkv(r, k, v, lw, kk, kka, rw_r_k[j].reshape(1, d), rw_ln_w[j].reshape(1, d),
                     rw_ln_b[j].reshape(1, d), bsz, seq)
            xf = _rwkv_out(y, g, xf, gt1, bf(rw_w_o[j]), seq)
        xf = _mlp(xf, a2, sh2, gt2, fg, bf(mlp_w1[i]), bf(mlp_w2[i]), seq, i == depth - 1)
    return xf.reshape(bsz, seq, d)
```

```python
import functools

import jax
import jax.numpy as jnp
from jax import lax
from jax.experimental import pallas as pl
from jax.experimental.pallas import tpu as pltpu

F32 = jnp.float32
BF16 = jnp.bfloat16

HEAD_SIZE = 64
WKV_CHUNK = 64
GROUP_LANES = 256
VREG_LANES = 128
WKV_PAIR = 2
NORM_EPS = 1e-6
GN_EPS = 64e-5
VMEM_LIMIT_BYTES = 56 * 1024 * 1024

MLP_TM = 1024
MLP_TF = 1024
CONV_TM = 512
RWKV_IN_TM = 256
WKV_TT = 256
RWKV_OUT_TM = 512


def _params(*semantics):
    return pltpu.CompilerParams(dimension_semantics=semantics,
                                vmem_limit_bytes=VMEM_LIMIT_BYTES)


def _dot(a, b):
    return jnp.dot(a, b, preferred_element_type=F32)


def _dot_nt(a, b):
    return lax.dot_general(a, b, (((1,), (1,)), ((), ())), preferred_element_type=F32)


def _sigmoid(x):
    return 1.0 / (1.0 + jnp.exp(-x))


def _norm_mod(x, a, s):
    ms = jnp.mean(x * x, axis=-1, keepdims=True)
    return x * lax.rsqrt(ms + NORM_EPS) * a + s


def _split2(x):
    hi = x.astype(BF16)
    lo = (x - hi.astype(F32)).astype(BF16)
    return hi, lo


def _head_sum(x, e):
    hi, lo = _split2(x)
    return _dot(hi, e) + _dot(lo, e)


def _head_bcast(s, et):
    hi, lo = _split2(s)
    return _dot(hi, et) + _dot(lo, et)


def _head_indicator(d):
    h = d // HEAD_SIZE
    e = (lax.broadcasted_iota(jnp.int32, (d, h), 0) // HEAD_SIZE
         == lax.broadcasted_iota(jnp.int32, (d, h), 1))
    et = (lax.broadcasted_iota(jnp.int32, (h, d), 1) // HEAD_SIZE
          == lax.broadcasted_iota(jnp.int32, (h, d), 0))
    return (jnp.where(e, 1.0, 0.0).astype(BF16), jnp.where(et, 1.0, 0.0).astype(BF16))


def _ada_kernel(c_ref, w_ref, b_ref, o_ref):
    c = c_ref[...]
    ca = (c * _sigmoid(c)).astype(BF16)
    o_ref[0] = _dot(ca, w_ref[0].astype(BF16)) + b_ref[0]


def _ada_mod(c, ada_w, ada_b):
    depth, d, n = ada_w.shape
    bsz = c.shape[0]
    rows = 16
    cp = jnp.zeros((rows, d), F32).at[:bsz].set(c)
    tn = min(n, 2048)
    out = pl.pallas_call(
        _ada_kernel,
        out_shape=jax.ShapeDtypeStruct((depth, rows, n), F32),
        grid=(depth, n // tn),
        in_specs=[pl.BlockSpec((rows, d), lambda i, j: (0, 0)),
                  pl.BlockSpec((1, d, tn), lambda i, j: (i, 0, j)),
                  pl.BlockSpec((1, 1, tn), lambda i, j: (i, 0, j))],
        out_specs=pl.BlockSpec((1, rows, tn), lambda i, j: (i, 0, j)),
        compiler_params=_params("parallel", "parallel"),
        name="ada_mod",
    )(cp, ada_w, ada_b.reshape(depth, 1, n))
    return out[:, :bsz]


def _mlp_kernel(x_ref, a_ref, s_ref, g_ref, fg_ref, w1_ref, w2_ref, o_ref, h_ref, acc_ref,
                *, final_norm):
    f = pl.program_id(1)

    @pl.when(f == 0)
    def _():
        h_ref[...] = _norm_mod(x_ref[...], a_ref[0], s_ref[0]).astype(BF16)
        acc_ref[...] = jnp.zeros_like(acc_ref)

    z = jnp.maximum(_dot(h_ref[...], w1_ref[...]), 0.0)
    acc_ref[...] += _dot((z * z).astype(BF16), w2_ref[...])

    @pl.when(f == pl.num_programs(1) - 1)
    def _():
        y = x_ref[...] + g_ref[0] * acc_ref[...]
        if final_norm:
            ms = jnp.mean(y * y, axis=-1, keepdims=True)
            y = y * lax.rsqrt(ms + NORM_EPS) * fg_ref[...]
        o_ref[...] = y


def _mlp(x, a, s, g, fg, w1, w2, seq, final_norm):
    m, d = x.shape
    ff = w1.shape[1]
    tm = min(seq, MLP_TM)
    tf = min(ff, MLP_TF)
    per_seq = seq // tm
    vec = pl.BlockSpec((1, 1, d), lambda i, f: (i // per_seq, 0, 0))
    return pl.pallas_call(
        functools.partial(_mlp_kernel, final_norm=final_norm),
        out_shape=jax.ShapeDtypeStruct((m, d), F32),
        grid=(m // tm, ff // tf),
        in_specs=[pl.BlockSpec((tm, d), lambda i, f: (i, 0)), vec, vec, vec,
                  pl.BlockSpec((1, d), lambda i, f: (0, 0)),
                  pl.BlockSpec((d, tf), lambda i, f: (0, f)),
                  pl.BlockSpec((tf, d), lambda i, f: (f, 0))],
        out_specs=pl.BlockSpec((tm, d), lambda i, f: (i, 0)),
        scratch_shapes=[pltpu.VMEM((tm, d), BF16), pltpu.VMEM((tm, d), F32)],
        compiler_params=_params("parallel", "arbitrary"),
        name="mlp",
    )(x, a, s, g, fg, w1, w2)


def _conv_in_kernel(x_ref, a_ref, s_ref, wb_ref, wc_ref, wh_ref, b_ref, u_ref):
    h = _norm_mod(x_ref[...], a_ref[0], s_ref[0]).astype(BF16)
    b_ref[...] = _dot(h, wb_ref[...])
    u_ref[...] = _dot(h, wc_ref[...]) * _dot(h, wh_ref[...])


def _conv_in(x, a, s, wb, wc, wh, seq):
    m, d = x.shape
    tm = min(seq, CONV_TM)
    per_seq = seq // tm
    vec = pl.BlockSpec((1, 1, d), lambda i: (i // per_seq, 0, 0))
    wspec = pl.BlockSpec((d, d), lambda i: (0, 0))
    tile = pl.BlockSpec((tm, d), lambda i: (i, 0))
    return pl.pallas_call(
        _conv_in_kernel,
        out_shape=(jax.ShapeDtypeStruct((m, d), F32), jax.ShapeDtypeStruct((m, d), F32)),
        grid=(m // tm,),
        in_specs=[tile, vec, vec, wspec, wspec, wspec],
        out_specs=(tile, tile),
        compiler_params=_params("parallel"),
        name="conv_in",
    )(x, a, s, wb, wc, wh)


def _conv_out_kernel(u_ref, b_ref, x_ref, cw_ref, g_ref, wo_ref, o_ref, carry_ref, *, per_seq):
    i = pl.program_id(0)
    tm = u_ref.shape[0]

    @pl.when(i % per_seq == 0)
    def _():
        carry_ref[...] = jnp.zeros_like(carry_ref)

    u = u_ref[...]
    prev = carry_ref[...]
    row = lax.broadcasted_iota(jnp.int32, (tm, 1), 0)
    u1 = jnp.where(row == 0, prev[7:8], pltpu.roll(u, 1, 0))
    u2 = jnp.where(row == 0, prev[6:7], jnp.where(row == 1, prev[7:8], pltpu.roll(u, 2, 0)))
    cw = cw_ref[...]
    conv = cw[0:1] * u2 + cw[1:2] * u1 + cw[2:3] * u
    y = (b_ref[...] * conv).astype(BF16)
    o_ref[...] = x_ref[...] + g_ref[0] * _dot(y, wo_ref[...])
    carry_ref[...] = u[tm - 8:tm]


def _conv_out(u, b, x, cw, g, wo, seq):
    m, d = x.shape
    tm = min(seq, CONV_TM)
    per_seq = seq // tm
    tile = pl.BlockSpec((tm, d), lambda i: (i, 0))
    return pl.pallas_call(
        functools.partial(_conv_out_kernel, per_seq=per_seq),
        out_shape=jax.ShapeDtypeStruct((m, d), F32),
        grid=(m // tm,),
        in_specs=[tile, tile, tile,
                  pl.BlockSpec(cw.shape, lambda i: (0, 0)),
                  pl.BlockSpec((1, 1, d), lambda i: (i // per_seq, 0, 0)),
                  pl.BlockSpec((d, d), lambda i: (0, 0))],
        out_specs=tile,
        scratch_shapes=[pltpu.VMEM((8, d), F32)],
        compiler_params=_params("arbitrary"),
        name="conv_out",
    )(u, b, x, cw, g, wo)


def _rwkv_in_kernel(*refs, per_seq, use_vres):
    if use_vres:
        (x_ref, a_ref, s_ref, mu_ref, vec_ref, wr_ref, wk_ref, wv_ref, w1_ref, w2_ref,
         a1_ref, a2_ref, g1_ref, g2_ref, vf_ref, v1_ref, v2_ref,
         r_ref, k_ref, v_ref, lw_ref, kk_ref, kka_ref, g_ref, carry_ref) = refs
    else:
        (x_ref, a_ref, s_ref, mu_ref, vec_ref, wr_ref, wk_ref, wv_ref, w1_ref, w2_ref,
         a1_ref, a2_ref, g1_ref, g2_ref,
         r_ref, k_ref, v_ref, lw_ref, kk_ref, kka_ref, g_ref, carry_ref) = refs
    i = pl.program_id(0)
    tm, d = x_ref.shape

    @pl.when(i % per_seq == 0)
    def _():
        carry_ref[...] = jnp.zeros_like(carry_ref)

    h = _norm_mod(x_ref[...], a_ref[0], s_ref[0])
    prev = carry_ref[...]
    row = lax.broadcasted_iota(jnp.int32, (tm, 1), 0)
    xx = jnp.where(row == 0, prev[7:8], pltpu.roll(h, 1, 0)) - h
    carry_ref[...] = h[tm - 8:tm]

    mu = mu_ref[...]
    vec = vec_ref[...]
    def mix(n):
        return (h + xx * mu[n:n + 1]).astype(BF16)
    xr, xw, xk, xv, xa, xg = (mix(n) for n in range(6))

    r = _dot(xr, wr_ref[...])
    k = _dot(xk, wk_ref[...])
    v = _dot(xv, wv_ref[...])
    wl = vec[0:1] + _dot(jnp.tanh(_dot(xw, w1_ref[...])).astype(BF16), w2_ref[...])
    a = _sigmoid(vec[1:2] + _dot(_dot(xa, a1_ref[...]).astype(BF16), a2_ref[...]))
    g = _dot(_sigmoid(_dot(xg, g1_ref[...])).astype(BF16), g2_ref[...])
    if use_vres:
        mix_v = _sigmoid(vec[4:5] + _dot(_dot(xv, v1_ref[...]).astype(BF16), v2_ref[...]))
        v = v + (vf_ref[...] - v) * mix_v

    lw_ref[...] = -jnp.exp(-0.5) * _sigmoid(wl)

    e, et = _head_indicator(d)
    kk = k * vec[2:3]
    norm = jnp.maximum(jnp.sqrt(_head_sum(kk * kk, e)), 1e-12)
    kk = kk * _head_bcast(1.0 / norm, et)
    r_ref[...] = r
    k_ref[...] = k * (1.0 + (a - 1.0) * vec[3:4])
    v_ref[...] = v
    kk_ref[...] = kk
    kka_ref[...] = kk * a
    g_ref[...] = g


def _rwkv_in(x, a, s, mu, vec, weights, vfirst, vres_w, seq):
    m, d = x.shape
    tm = min(seq, RWKV_IN_TM)
    per_seq = seq // tm
    use_vres = vfirst is not None
    tile = pl.BlockSpec((tm, d), lambda i: (i, 0))
    mvec = pl.BlockSpec((1, 1, d), lambda i: (i // per_seq, 0, 0))

    def full(arr):
        return pl.BlockSpec(arr.shape, lambda i: (0,) * arr.ndim)

    args = [x, a, s, mu, vec] + list(weights)
    specs = [tile, mvec, mvec, full(mu), full(vec)] + [full(w) for w in weights]
    if use_vres:
        args += [vfirst] + list(vres_w)
        specs += [tile] + [full(w) for w in vres_w]
    out = jax.ShapeDtypeStruct((m, d), F32)
    return pl.pallas_call(
        functools.partial(_rwkv_in_kernel, per_seq=per_seq, use_vres=use_vres),
        out_shape=(out,) * 7,
        grid=(m // tm,),
        in_specs=specs,
        out_specs=(tile,) * 7,
        scratch_shapes=[pltpu.VMEM((8, d), F32)],
        compiler_params=_params("arbitrary"),
        name="rwkv_in",
    )(*args)


def _block_diag(x):
    n_rows, width = x.shape
    xb = x.astype(BF16)
    lane_head = lax.broadcasted_iota(jnp.int32, (n_rows, VREG_LANES), 1) // HEAD_SIZE
    zero = jnp.zeros((n_rows, VREG_LANES), BF16)
    n_tiles = width // VREG_LANES
    blocks = []
    for h in range(width // HEAD_SIZE):
        tile, sub = divmod(h * HEAD_SIZE, VREG_LANES)
        piece = jnp.where(lane_head == sub // HEAD_SIZE,
                          xb[:, tile * VREG_LANES:(tile + 1) * VREG_LANES], zero)
        blocks.append(jnp.concatenate([piece if n == tile else zero for n in range(n_tiles)], axis=1))
    return jnp.concatenate(blocks, axis=0)


def _wkv_kernel(r_ref, k_ref, v_ref, lw_ref, kk_ref, kka_ref, rk_ref, lnw_ref, lnb_ref,
                o_ref, state_ref, y_ref, wr_s, u0_s, arb_s, ark_s, bkt_s, gm_s):
    tt, d = r_ref.shape
    c_len, gl = WKV_CHUNK, GROUP_LANES
    n_chunks = tt // c_len
    n_groups = d // gl
    n_sq = (c_len - 1).bit_length() - 1

    @pl.when(pl.program_id(1) == 0)
    def _():
        state_ref[...] = jnp.zeros_like(state_ref)

    row = lax.broadcasted_iota(jnp.int32, (c_len, gl), 0)
    col = lax.broadcasted_iota(jnp.int32, (c_len, gl), 1) % c_len
    strict = col < row
    incl = col <= row
    eye = jnp.where(col == row, 1.0, 0.0)
    bd_mask = (lax.broadcasted_iota(jnp.int32, (gl, gl), 0) // HEAD_SIZE
               == lax.broadcasted_iota(jnp.int32, (gl, gl), 1) // HEAD_SIZE)
    tri = jnp.where(lax.broadcasted_iota(jnp.int32, (c_len, c_len), 1)
                    <= lax.broadcasted_iota(jnp.int32, (c_len, c_len), 0), 1.0, 0.0).astype(BF16)
    tri3 = jnp.concatenate([tri, tri, tri], axis=1)

    def bf(x):
        return x.astype(BF16)

    def lanes(gi):
        return slice(gi * gl, (gi + 1) * gl)

    def prepare(cp, _):
        items = [(ci, gi) for ci in range(WKV_PAIR) for gi in range(n_groups)]

        def rows(ci):
            return pl.ds(pl.multiple_of((cp * WKV_PAIR + ci) * c_len, c_len), c_len)

        def slot(ci, gi):
            return (cp * WKV_PAIR + ci) * n_groups + gi

        cums = []
        for ci, gi in items:
            lw = lw_ref[rows(ci), lanes(gi)]
            p0 = lw.astype(BF16)
            rem = lw - p0.astype(F32)
            p1 = rem.astype(BF16)
            p2 = (rem - p1.astype(F32)).astype(BF16)
            cums.append(_dot(tri3, jnp.concatenate([p0, p1, p2], axis=0)))

        at, lhs, bt, kt = [], [], [], []
        for (ci, gi), cum in zip(items, cums):
            rs, ls = rows(ci), lanes(gi)
            c_last = cum[c_len - 1:c_len]
            kka = kka_ref[rs, ls]
            kx = k_ref[rs, ls]
            e_inv = jnp.exp(-cum)
            e_tail = jnp.exp(c_last - cum)
            a_t = -kk_ref[rs, ls] * jnp.exp(cum - lw_ref[rs, ls])
            r_t = r_ref[rs, ls] * jnp.exp(cum)
            at.append(bf(a_t))
            lhs.append(bf(jnp.concatenate([a_t, r_t], axis=0)))
            bt.append(bf(kka * e_inv))
            kt.append(bf(kx * e_inv))
            bkt_s[slot(ci, gi)] = bf(jnp.concatenate([kka * e_tail, kx * e_tail], axis=0).T)
            gm_s[slot(ci, gi)] = jnp.broadcast_to(jnp.exp(c_last), (VREG_LANES, gl)).T

        sb = [_dot_nt(l, _block_diag(b)) for l, b in zip(lhs, bt)]
        sk = [_dot_nt(l, _block_diag(k)) for l, k in zip(lhs, kt)]
        for (ci, gi), b, k in zip(items, sb, sk):
            arb_s[slot(ci, gi)] = bf(jnp.where(incl, b[c_len:], 0.0))
            ark_s[slot(ci, gi)] = bf(jnp.where(incl, k[c_len:], 0.0))

        xs = [jnp.where(strict, b[:c_len], 0.0) for b in sb]
        ts = [eye + x for x in xs]
        xs = [_dot(bf(x), _block_diag(x)) for x in xs]
        for _ in range(n_sq - 1):
            res = [_dot(bf(jnp.concatenate([x, t], axis=0)), _block_diag(x)) for x, t in zip(xs, ts)]
            xs = [r[:c_len] for r in res]
            ts = [t + r[c_len:] for t, r in zip(ts, res)]
        ts = [bf(t + _dot(bf(t), _block_diag(x))) for x, t in zip(xs, ts)]

        akv = [_dot(bf(jnp.where(strict, k[:c_len], 0.0)), _block_diag(v_ref[rows(ci), lanes(gi)]))
               for (ci, gi), k in zip(items, sk)]
        ws = [_dot(t, _block_diag(a)) for t, a in zip(ts, at)]
        u0 = [_dot(t, _block_diag(a)) for t, a in zip(ts, akv)]
        for (ci, gi), w, u, l in zip(items, ws, u0, lhs):
            wr_s[slot(ci, gi)] = jnp.concatenate([bf(w), l[c_len:]], axis=0)
            u0_s[slot(ci, gi)] = u
        return 0

    lax.fori_loop(0, n_chunks // WKV_PAIR, prepare, 0)

    def advance(c, _):
        rs = pl.ds(pl.multiple_of(c * c_len, c_len), c_len)
        slots = [c * n_groups + gi for gi in range(n_groups)]
        st = [state_ref[gi] for gi in range(n_groups)]
        vs = [v_ref[rs, lanes(gi)] for gi in range(n_groups)]
        ps = [_dot(wr_s[s], bf(x)) for s, x in zip(slots, st)]
        us = [u0_s[s] + p[:c_len] for s, p in zip(slots, ps)]
        upd = [_dot(bkt_s[s], bf(jnp.concatenate([u, v], axis=0))) for s, u, v in zip(slots, us, vs)]
        for gi in range(n_groups):
            gm = gm_s[slots[gi]]
            state_ref[gi] = (st[gi] * jnp.concatenate([gm] * (gl // VREG_LANES), axis=1)
                             + jnp.where(bd_mask, upd[gi], 0.0))
        for gi in range(n_groups):
            y_ref[rs, lanes(gi)] = (ps[gi][c_len:] + _dot(arb_s[slots[gi]], _block_diag(us[gi]))
                                    + _dot(ark_s[slots[gi]], _block_diag(vs[gi])))
        return 0

    lax.fori_loop(0, n_chunks, advance, 0)

    e, et = _head_indicator(d)
    y = y_ref[...]
    inv_n = 1.0 / HEAD_SIZE
    mean = _head_bcast(_head_sum(y, e) * inv_n, et)
    dev = y - mean
    var = _head_sum(dev * dev, e) * inv_n
    yn = dev * _head_bcast(lax.rsqrt(var + GN_EPS), et) * lnw_ref[...] + lnb_ref[...]
    bonus = _head_bcast(_head_sum(r_ref[...] * k_ref[...] * rk_ref[...], e), et) * v_ref[...]
    o_ref[...] = yn + bonus


def _wkv(r, k, v, lw, kk, kka, rk, lnw, lnb, bsz, seq):
    m, d = r.shape
    tt = min(seq, WKV_TT)
    assert tt % (WKV_CHUNK * WKV_PAIR) == 0
    per_seq = seq // tt
    n_groups = d // GROUP_LANES
    n_slots = tt // WKV_CHUNK * n_groups
    tile = pl.BlockSpec((tt, d), lambda b, t: (b * per_seq + t, 0))
    vec = pl.BlockSpec((1, d), lambda b, t: (0, 0))
    return pl.pallas_call(
        _wkv_kernel,
        out_shape=jax.ShapeDtypeStruct((m, d), F32),
        grid=(bsz, per_seq),
        in_specs=[tile] * 6 + [vec] * 3,
        out_specs=tile,
        scratch_shapes=[pltpu.VMEM((n_groups, GROUP_LANES, GROUP_LANES), F32),
                        pltpu.VMEM((tt, d), F32),
                        pltpu.VMEM((n_slots, 2 * WKV_CHUNK, GROUP_LANES), BF16),
                        pltpu.VMEM((n_slots, WKV_CHUNK, GROUP_LANES), F32),
                        pltpu.VMEM((n_slots, WKV_CHUNK, GROUP_LANES), BF16),
                        pltpu.VMEM((n_slots, WKV_CHUNK, GROUP_LANES), BF16),
                        pltpu.VMEM((n_slots, GROUP_LANES, 2 * WKV_CHUNK), BF16),
                        pltpu.VMEM((n_slots, GROUP_LANES, VREG_LANES), F32)],
        compiler_params=_params("parallel", "arbitrary"),
        name="wkv",
    )(r, k, v, lw, kk, kka, rk, lnw, lnb)


def _rwkv_out_kernel(y_ref, g_ref, x_ref, gate_ref, wo_ref, o_ref):
    z = (y_ref[...] * g_ref[...]).astype(BF16)
    o_ref[...] = x_ref[...] + gate_ref[0] * _dot(z, wo_ref[...])


def _rwkv_out(y, g, x, gate, wo, seq):
    m, d = x.shape
    tm = min(seq, RWKV_OUT_TM)
    per_seq = seq // tm
    tile = pl.BlockSpec((tm, d), lambda i: (i, 0))
    return pl.pallas_call(
        _rwkv_out_kernel,
        out_shape=jax.ShapeDtypeStruct((m, d), F32),
        grid=(m // tm,),
        in_specs=[tile, tile, tile,
                  pl.BlockSpec((1, 1, d), lambda i: (i // per_seq, 0, 0)),
                  pl.BlockSpec((d, d), lambda i: (0, 0))],
        out_specs=tile,
        compiler_params=_params("parallel"),
        name="rwkv_out",
    )(y, g, x, gate, wo)


def kernel(x, c, norm_g, final_g, ada_w, ada_b, conv_w_in, conv_w, conv_w_out, rw_mu, rw_w_rkv, rw_w_o, rw_w0, rw_w1, rw_w2, rw_a0, rw_a1, rw_a2, rw_g1, rw_g2, rw_k_k, rw_k_a, rw_r_k, rw_ln_w, rw_ln_b, rw_v0, rw_v1, rw_v2, mlp_w1, mlp_w2):
    bsz, seq, d = x.shape
    depth = ada_w.shape[0]
    assert d % GROUP_LANES == 0 and seq % WKV_CHUNK == 0
    bf = lambda w: w.astype(BF16)

    mod = _ada_mod(c, ada_w, ada_b).reshape(depth, bsz, 6, 1, d)
    fg = final_g.reshape(1, d)
    xf = x.reshape(bsz * seq, d)
    v_first = None
    for i in range(depth):
        sh1, sc1, gt1, sh2, sc2, gt2 = (mod[i, :, n] for n in range(6))
        a1 = norm_g[i, 0] * (1.0 + sc1)
        a2 = norm_g[i, 1] * (1.0 + sc2)
        j = i // 2
        if i % 2 == 0:
            w_in = bf(conv_w_in[j])
            b_gate, u = _conv_in(xf, a1, sh1, w_in[:, :d], w_in[:, d:2 * d], w_in[:, 2 * d:], seq)
            xf = _conv_out(u, b_gate, xf, conv_w[j], gt1, bf(conv_w_out[j]), seq)
        else:
            vec = jnp.stack([rw_w0[j], rw_a0[j], rw_k_k[j], rw_k_a[j],
                             rw_v0[j - 1] if v_first is not None else jnp.zeros((d,), F32)])
            weights = [bf(rw_w_rkv[j, 0]), bf(rw_w_rkv[j, 1]), bf(rw_w_rkv[j, 2]),
                       bf(rw_w1[j]), bf(rw_w2[j]), bf(rw_a1[j]), bf(rw_a2[j]),
                       bf(rw_g1[j]), bf(rw_g2[j])]
            vres_w = None if v_first is None else [bf(rw_v1[j - 1]), bf(rw_v2[j - 1])]
            r, k, v, lw, kk, kka, g = _rwkv_in(xf, a1, sh1, rw_mu[j], vec, weights,
                                               v_first, vres_w, seq)
            if v_first is None:
                v_first = v
            y = _wkv(r, k, v, lw, kk, kka, rw_r_k[j].reshape(1, d), rw_ln_w[j].reshape(1, d),
                     rw_ln_b[j].reshape(1, d), bsz, seq)
            xf = _rwkv_out(y, g, xf, gt1, bf(rw_w_o[j]), seq)
        xf = _mlp(xf, a2, sh2, gt2, fg, bf(mlp_w1[i]), bf(mlp_w2[i]), seq, i == depth - 1)
    return xf.reshape(bsz, seq, d)
```

```python
import functools

import jax
import jax.numpy as jnp
from jax import lax
from jax.experimental import pallas as pl
from jax.experimental.pallas import tpu as pltpu

F32 = jnp.float32
BF16 = jnp.bfloat16

HEAD_SIZE = 64
WKV_CHUNK = 64
GROUP_LANES = 256
VREG_LANES = 128
WKV_PAIR = 2
NORM_EPS = 1e-6
GN_EPS = 64e-5
VMEM_LIMIT_BYTES = 56 * 1024 * 1024

MLP_TM = 1024
MLP_TF = 1024
CONV_TM = 512
RWKV_IN_TM = 256
WKV_TT = 512
RWKV_OUT_TM = 512


def _params(*semantics):
    return pltpu.CompilerParams(dimension_semantics=semantics,
                                vmem_limit_bytes=VMEM_LIMIT_BYTES)


def _dot(a, b):
    return jnp.dot(a, b, preferred_element_type=F32)


def _dot_nt(a, b):
    return lax.dot_general(a, b, (((1,), (1,)), ((), ())), preferred_element_type=F32)


def _sigmoid(x):
    return 1.0 / (1.0 + jnp.exp(-x))


def _norm_mod(x, a, s):
    ms = jnp.mean(x * x, axis=-1, keepdims=True)
    return x * lax.rsqrt(ms + NORM_EPS) * a + s


def _split2(x):
    hi = x.astype(BF16)
    lo = (x - hi.astype(F32)).astype(BF16)
    return hi, lo


def _head_sum(x, e):
    hi, lo = _split2(x)
    return _dot(hi, e) + _dot(lo, e)


def _head_bcast(s, et):
    hi, lo = _split2(s)
    return _dot(hi, et) + _dot(lo, et)


def _head_indicator(d):
    h = d // HEAD_SIZE
    e = (lax.broadcasted_iota(jnp.int32, (d, h), 0) // HEAD_SIZE
         == lax.broadcasted_iota(jnp.int32, (d, h), 1))
    et = (lax.broadcasted_iota(jnp.int32, (h, d), 1) // HEAD_SIZE
          == lax.broadcasted_iota(jnp.int32, (h, d), 0))
    return (jnp.where(e, 1.0, 0.0).astype(BF16), jnp.where(et, 1.0, 0.0).astype(BF16))


def _shift_rows(x, prev, n):
    row = lax.broadcasted_iota(jnp.int32, (x.shape[0], 1), 0)
    out = pltpu.roll(x, n, 0)
    for j in range(n):
        out = jnp.where(row == j, prev[8 - n + j:8 - n + j + 1], out)
    return out


def _ada_kernel(c_ref, w_ref, b_ref, o_ref):
    c = c_ref[...]
    ca = (c * _sigmoid(c)).astype(BF16)
    o_ref[0] = _dot(ca, w_ref[0].astype(BF16)) + b_ref[0]


def _ada_mod(c, ada_w, ada_b):
    depth, d, n = ada_w.shape
    bsz = c.shape[0]
    rows = 16
    cp = jnp.zeros((rows, d), F32).at[:bsz].set(c)
    tn = min(n, 2048)
    out = pl.pallas_call(
        _ada_kernel,
        out_shape=jax.ShapeDtypeStruct((depth, rows, n), F32),
        grid=(depth, n // tn),
        in_specs=[pl.BlockSpec((rows, d), lambda i, j: (0, 0)),
                  pl.BlockSpec((1, d, tn), lambda i, j: (i, 0, j)),
                  pl.BlockSpec((1, 1, tn), lambda i, j: (i, 0, j))],
        out_specs=pl.BlockSpec((1, rows, tn), lambda i, j: (i, 0, j)),
        compiler_params=_params("parallel", "parallel"),
        name="ada_mod",
    )(cp, ada_w, ada_b.reshape(depth, 1, n))
    return out[:, :bsz]


def _mlp_kernel(x_ref, a_ref, s_ref, g_ref, fg_ref, w1_ref, w2_ref, o_ref, h_ref, acc_ref,
                *, final_norm):
    f = pl.program_id(1)

    @pl.when(f == 0)
    def _():
        h_ref[...] = _norm_mod(x_ref[...], a_ref[0], s_ref[0]).astype(BF16)
        acc_ref[...] = jnp.zeros_like(acc_ref)

    z = jnp.maximum(_dot(h_ref[...], w1_ref[...]), 0.0)
    acc_ref[...] += _dot((z * z).astype(BF16), w2_ref[...])

    @pl.when(f == pl.num_programs(1) - 1)
    def _():
        y = x_ref[...] + g_ref[0] * acc_ref[...]
        if final_norm:
            ms = jnp.mean(y * y, axis=-1, keepdims=True)
            y = y * lax.rsqrt(ms + NORM_EPS) * fg_ref[...]
        o_ref[...] = y


def _mlp(x, a, s, g, fg, w1, w2, seq, final_norm):
    m, d = x.shape
    ff = w1.shape[1]
    tm = min(seq, MLP_TM)
    tf = min(ff, MLP_TF)
    per_seq = seq // tm
    vec = pl.BlockSpec((1, 1, d), lambda i, f: (i // per_seq, 0, 0))
    return pl.pallas_call(
        functools.partial(_mlp_kernel, final_norm=final_norm),
        out_shape=jax.ShapeDtypeStruct((m, d), F32),
        grid=(m // tm, ff // tf),
        in_specs=[pl.BlockSpec((tm, d), lambda i, f: (i, 0)), vec, vec, vec,
                  pl.BlockSpec((1, d), lambda i, f: (0, 0)),
                  pl.BlockSpec((d, tf), lambda i, f: (0, f)),
                  pl.BlockSpec((tf, d), lambda i, f: (f, 0))],
        out_specs=pl.BlockSpec((tm, d), lambda i, f: (i, 0)),
        scratch_shapes=[pltpu.VMEM((tm, d), BF16), pltpu.VMEM((tm, d), F32)],
        compiler_params=_params("parallel", "arbitrary"),
        name="mlp",
    )(x, a, s, g, fg, w1, w2)


def _conv_kernel(x_ref, a_ref, s_ref, g_ref, win_ref, cw_ref, wo_ref, o_ref, carry_ref, *, per_seq):
    i = pl.program_id(0)
    tm, d = x_ref.shape

    @pl.when(i % per_seq == 0)
    def _():
        carry_ref[...] = jnp.zeros_like(carry_ref)

    x = x_ref[...]
    h = _norm_mod(x, a_ref[0], s_ref[0]).astype(BF16)
    u = _dot(h, win_ref[:, d:2 * d]) * _dot(h, win_ref[:, 2 * d:])
    prev = carry_ref[...]
    cw = cw_ref[...]
    conv = cw[0:1] * _shift_rows(u, prev, 2) + cw[1:2] * _shift_rows(u, prev, 1) + cw[2:3] * u
    carry_ref[...] = u[tm - 8:tm]
    y = (_dot(h, win_ref[:, :d]) * conv).astype(BF16)
    o_ref[...] = x + g_ref[0] * _dot(y, wo_ref[...])


def _conv_mixer(x, a, s, g, w_in, cw, wo, seq):
    m, d = x.shape
    tm = min(seq, CONV_TM)
    per_seq = seq // tm
    tile = pl.BlockSpec((tm, d), lambda i: (i, 0))
    vec = pl.BlockSpec((1, 1, d), lambda i: (i // per_seq, 0, 0))
    return pl.pallas_call(
        functools.partial(_conv_kernel, per_seq=per_seq),
        out_shape=jax.ShapeDtypeStruct((m, d), F32),
        grid=(m // tm,),
        in_specs=[tile, vec, vec, vec,
                  pl.BlockSpec(w_in.shape, lambda i: (0, 0)),
                  pl.BlockSpec(cw.shape, lambda i: (0, 0)),
                  pl.BlockSpec((d, d), lambda i: (0, 0))],
        out_specs=tile,
        scratch_shapes=[pltpu.VMEM((8, d), F32)],
        compiler_params=_params("arbitrary"),
        name="conv_mixer",
    )(x, a, s, g, w_in, cw, wo)


def _rwkv_in_kernel(*refs, per_seq, use_vres):
    if use_vres:
        (x_ref, a_ref, s_ref, mu_ref, vec_ref, wr_ref, wk_ref, wv_ref, w1_ref, w2_ref,
         a1_ref, a2_ref, g1_ref, g2_ref, vf_ref, v1_ref, v2_ref,
         r_ref, k_ref, v_ref, lw_ref, kk_ref, kka_ref, g_ref, carry_ref) = refs
    else:
        (x_ref, a_ref, s_ref, mu_ref, vec_ref, wr_ref, wk_ref, wv_ref, w1_ref, w2_ref,
         a1_ref, a2_ref, g1_ref, g2_ref,
         r_ref, k_ref, v_ref, lw_ref, kk_ref, kka_ref, g_ref, carry_ref) = refs
    i = pl.program_id(0)
    tm, d = x_ref.shape

    @pl.when(i % per_seq == 0)
    def _():
        carry_ref[...] = jnp.zeros_like(carry_ref)

    h = _norm_mod(x_ref[...], a_ref[0], s_ref[0])
    xx = _shift_rows(h, carry_ref[...], 1) - h
    carry_ref[...] = h[tm - 8:tm]

    mu = mu_ref[...]
    vec = vec_ref[...]
    def mix(n):
        return (h + xx * mu[n:n + 1]).astype(BF16)
    xr, xw, xk, xv, xa, xg = (mix(n) for n in range(6))

    r = _dot(xr, wr_ref[...])
    k = _dot(xk, wk_ref[...])
    v = _dot(xv, wv_ref[...])
    wl = vec[0:1] + _dot(jnp.tanh(_dot(xw, w1_ref[...])).astype(BF16), w2_ref[...])
    a = _sigmoid(vec[1:2] + _dot(_dot(xa, a1_ref[...]).astype(BF16), a2_ref[...]))
    g = _dot(_sigmoid(_dot(xg, g1_ref[...])).astype(BF16), g2_ref[...])
    if use_vres:
        mix_v = _sigmoid(vec[4:5] + _dot(_dot(xv, v1_ref[...]).astype(BF16), v2_ref[...]))
        v = v + (vf_ref[...] - v) * mix_v

    lw_ref[...] = -jnp.exp(-0.5) * _sigmoid(wl)

    e, et = _head_indicator(d)
    kk = k * vec[2:3]
    norm = jnp.maximum(jnp.sqrt(_head_sum(kk * kk, e)), 1e-12)
    kk = kk * _head_bcast(1.0 / norm, et)
    r_ref[...] = r
    k_ref[...] = k * (1.0 + (a - 1.0) * vec[3:4])
    v_ref[...] = v
    kk_ref[...] = kk
    kka_ref[...] = kk * a
    g_ref[...] = g


def _rwkv_in(x, a, s, mu, vec, weights, vfirst, vres_w, seq):
    m, d = x.shape
    tm = min(seq, RWKV_IN_TM)
    per_seq = seq // tm
    use_vres = vfirst is not None
    tile = pl.BlockSpec((tm, d), lambda i: (i, 0))
    mvec = pl.BlockSpec((1, 1, d), lambda i: (i // per_seq, 0, 0))

    def full(arr):
        return pl.BlockSpec(arr.shape, lambda i: (0,) * arr.ndim)

    args = [x, a, s, mu, vec] + list(weights)
    specs = [tile, mvec, mvec, full(mu), full(vec)] + [full(w) for w in weights]
    if use_vres:
        args += [vfirst] + list(vres_w)
        specs += [tile] + [full(w) for w in vres_w]
    out = jax.ShapeDtypeStruct((m, d), F32)
    return pl.pallas_call(
        functools.partial(_rwkv_in_kernel, per_seq=per_seq, use_vres=use_vres),
        out_shape=(out,) * 7,
        grid=(m // tm,),
        in_specs=specs,
        out_specs=(tile,) * 7,
        scratch_shapes=[pltpu.VMEM((8, d), F32)],
        compiler_params=_params("arbitrary"),
        name="rwkv_in",
    )(*args)


def _block_diag(x):
    n_rows, width = x.shape
    xb = x.astype(BF16)
    lane_head = lax.broadcasted_iota(jnp.int32, (n_rows, VREG_LANES), 1) // HEAD_SIZE
    zero = jnp.zeros((n_rows, VREG_LANES), BF16)
    n_tiles = width // VREG_LANES
    blocks = []
    for h in range(width // HEAD_SIZE):
        tile, sub = divmod(h * HEAD_SIZE, VREG_LANES)
        piece = jnp.where(lane_head == sub // HEAD_SIZE,
                          xb[:, tile * VREG_LANES:(tile + 1) * VREG_LANES], zero)
        blocks.append(jnp.concatenate([piece if n == tile else zero for n in range(n_tiles)], axis=1))
    return jnp.concatenate(blocks, axis=0)


def _wkv_kernel(r_ref, k_ref, v_ref, lw_ref, kk_ref, kka_ref, rk_ref, lnw_ref, lnb_ref,
                o_ref, state_ref, wr_s, u0_s, arb_s, ark_s, bkt_s, gm_s):
    tt, d = r_ref.shape
    c_len, gl = WKV_CHUNK, GROUP_LANES
    n_pairs = tt // (c_len * WKV_PAIR)
    n_groups = d // gl
    n_sq = (c_len - 1).bit_length() - 1

    @pl.when(pl.program_id(1) == 0)
    def _():
        state_ref[...] = jnp.zeros_like(state_ref)

    row = lax.broadcasted_iota(jnp.int32, (c_len, gl), 0)
    col = lax.broadcasted_iota(jnp.int32, (c_len, gl), 1) % c_len
    strict = col < row
    incl = col <= row
    eye = jnp.where(col == row, 1.0, 0.0)
    bd_mask = (lax.broadcasted_iota(jnp.int32, (gl, gl), 0) // HEAD_SIZE
               == lax.broadcasted_iota(jnp.int32, (gl, gl), 1) // HEAD_SIZE)
    tri = jnp.where(lax.broadcasted_iota(jnp.int32, (c_len, c_len), 1)
                    <= lax.broadcasted_iota(jnp.int32, (c_len, c_len), 0), 1.0, 0.0).astype(BF16)
    tri3 = jnp.concatenate([tri, tri, tri], axis=1)
    head_ones = jnp.where(bd_mask, 1.0, 0.0).astype(BF16)
    inv_n = 1.0 / HEAD_SIZE

    def bf(x):
        return x.astype(BF16)

    def lanes(gi):
        return slice(gi * gl, (gi + 1) * gl)

    def chunk_rows(c):
        start = c * c_len
        return pl.ds(start if isinstance(start, int) else pl.multiple_of(start, c_len), c_len)

    def prepare(p):
        items = [(p * WKV_PAIR + ci, gi) for ci in range(WKV_PAIR) for gi in range(n_groups)]

        cums = []
        for c, gi in items:
            lw = lw_ref[chunk_rows(c), lanes(gi)]
            p0 = lw.astype(BF16)
            rem = lw - p0.astype(F32)
            p1 = rem.astype(BF16)
            p2 = (rem - p1.astype(F32)).astype(BF16)
            cums.append(_dot(tri3, jnp.concatenate([p0, p1, p2], axis=0)))
        yield

        at, lhs, bt, kt = [], [], [], []
        for (c, gi), cum in zip(items, cums):
            rs, ls, slot = chunk_rows(c), lanes(gi), c * n_groups + gi
            c_last = cum[c_len - 1:c_len]
            kka = kka_ref[rs, ls]
            kx = k_ref[rs, ls]
            e_inv = jnp.exp(-cum)
            e_tail = jnp.exp(c_last - cum)
            a_t = -kk_ref[rs, ls] * jnp.exp(cum - lw_ref[rs, ls])
            r_t = r_ref[rs, ls] * jnp.exp(cum)
            at.append(bf(a_t))
            lhs.append(bf(jnp.concatenate([a_t, r_t], axis=0)))
            bt.append(bf(kka * e_inv))
            kt.append(bf(kx * e_inv))
            bkt_s[slot] = bf(jnp.concatenate([kka * e_tail, kx * e_tail], axis=0).T)
            gm_s[slot] = jnp.broadcast_to(jnp.exp(c_last), (VREG_LANES, gl)).T

        sb = [_dot_nt(l, _block_diag(b)) for l, b in zip(lhs, bt)]
        sk = [_dot_nt(l, _block_diag(k)) for l, k in zip(lhs, kt)]
        for (c, gi), b, k in zip(items, sb, sk):
            arb_s[c * n_groups + gi] = bf(jnp.where(incl, b[c_len:], 0.0))
            ark_s[c * n_groups + gi] = bf(jnp.where(incl, k[c_len:], 0.0))
        yield

        xs = [jnp.where(strict, b[:c_len], 0.0) for b in sb]
        ts = [eye + x for x in xs]
        xs = [_dot(bf(x), _block_diag(x)) for x in xs]
        yield
        for _ in range(n_sq - 1):
            res = [_dot(bf(jnp.concatenate([x, t], axis=0)), _block_diag(x)) for x, t in zip(xs, ts)]
            xs = [r[:c_len] for r in res]
            ts = [t + r[c_len:] for t, r in zip(ts, res)]
            yield
        ts = [bf(t + _dot(bf(t), _block_diag(x))) for x, t in zip(xs, ts)]
        yield

        akv = [_dot(bf(jnp.where(strict, k[:c_len], 0.0)), _block_diag(v_ref[chunk_rows(c), lanes(gi)]))
               for (c, gi), k in zip(items, sk)]
        ws = [_dot(t, _block_diag(a)) for t, a in zip(ts, at)]
        u0 = [_dot(t, _block_diag(a)) for t, a in zip(ts, akv)]
        for (c, gi), w, u, l in zip(items, ws, u0, lhs):
            wr_s[c * n_groups + gi] = jnp.concatenate([bf(w), l[c_len:]], axis=0)
            u0_s[c * n_groups + gi] = u

    def advance_chunk(c):
        rs = chunk_rows(c)
        slots = [c * n_groups + gi for gi in range(n_groups)]
        st = [state_ref[gi] for gi in range(n_groups)]
        vs = [v_ref[rs, lanes(gi)] for gi in range(n_groups)]
        ps = [_dot(wr_s[s], bf(x)) for s, x in zip(slots, st)]
        yield
        us = [u0_s[s] + p[:c_len] for s, p in zip(slots, ps)]
        upd = [_dot(bkt_s[s], bf(jnp.concatenate([u, v], axis=0))) for s, u, v in zip(slots, us, vs)]
        for gi in range(n_groups):
            gm = gm_s[slots[gi]]
            state_ref[gi] = (st[gi] * jnp.concatenate([gm] * (gl // VREG_LANES), axis=1)
                             + jnp.where(bd_mask, upd[gi], 0.0))
        yield
        ys = [p[c_len:] + _dot(arb_s[s], _block_diag(u)) + _dot(ark_s[s], _block_diag(v))
              for s, p, u, v in zip(slots, ps, us, vs)]
        sums = [_dot(bf(jnp.concatenate(
                    [y, r_ref[rs, lanes(gi)] * k_ref[rs, lanes(gi)] * rk_ref[:, lanes(gi)]], axis=0)),
                    head_ones) for gi, y in enumerate(ys)]
        yield
        devs = [y - s[:c_len] * inv_n for y, s in zip(ys, sums)]
        var = [_dot(bf(dv * dv), head_ones) * inv_n for dv in devs]
        for gi in range(n_groups):
            ls = lanes(gi)
            o_ref[rs, ls] = (devs[gi] * lax.rsqrt(var[gi] + GN_EPS) * lnw_ref[:, ls] + lnb_ref[:, ls]
                             + sums[gi][c_len:] * vs[gi])

    def run(stages, order):
        for name in order:
            next(stages[name], None)

    assert WKV_PAIR == 2
    run({"p": prepare(0)}, "p" * (n_sq + 4))

    def body(p, carry):
        stages = {"p": prepare(p), "a": advance_chunk((p - 1) * WKV_PAIR),
                  "b": advance_chunk((p - 1) * WKV_PAIR + 1)}
        run(stages, "apapbppbpapapbpbp" + "p" * (n_sq - 5))
        return carry

    lax.fori_loop(1, n_pairs, body, 0)
    last = (n_pairs - 1) * WKV_PAIR
    run({"a": advance_chunk(last), "b": advance_chunk(last + 1)}, "aabababb")


def _wkv(r, k, v, lw, kk, kka, rk, lnw, lnb, bsz, seq):
    m, d = r.shape
    tt = min(seq, WKV_TT)
    assert tt % (WKV_CHUNK * WKV_PAIR) == 0
    per_seq = seq // tt
    n_groups = d // GROUP_LANES
    n_slots = tt // WKV_CHUNK * n_groups
    tile = pl.BlockSpec((tt, d), lambda b, t: (b * per_seq + t, 0))
    vec = pl.BlockSpec((1, d), lambda b, t: (0, 0))
    return pl.pallas_call(
        _wkv_kernel,
        out_shape=jax.ShapeDtypeStruct((m, d), F32),
        grid=(bsz, per_seq),
        in_specs=[tile] * 6 + [vec] * 3,
        out_specs=tile,
        scratch_shapes=[pltpu.VMEM((n_groups, GROUP_LANES, GROUP_LANES), F32),
                        pltpu.VMEM((n_slots, 2 * WKV_CHUNK, GROUP_LANES), BF16),
                        pltpu.VMEM((n_slots, WKV_CHUNK, GROUP_LANES), F32),
                        pltpu.VMEM((n_slots, WKV_CHUNK, GROUP_LANES), BF16),
                        pltpu.VMEM((n_slots, WKV_CHUNK, GROUP_LANES), BF16),
                        pltpu.VMEM((n_slots, GROUP_LANES, 2 * WKV_CHUNK), BF16),
                        pltpu.VMEM((n_slots, GROUP_LANES, VREG_LANES), F32)],
        compiler_params=_params("parallel", "arbitrary"),
        name="wkv",
    )(r, k, v, lw, kk, kka, rk, lnw, lnb)


def _rwkv_out_kernel(y_ref, g_ref, x_ref, gate_ref, wo_ref, o_ref):
    z = (y_ref[...] * g_ref[...]).astype(BF16)
    o_ref[...] = x_ref[...] + gate_ref[0] * _dot(z, wo_ref[...])


def _rwkv_out(y, g, x, gate, wo, seq):
    m, d = x.shape
    tm = min(seq, RWKV_OUT_TM)
    per_seq = seq // tm
    tile = pl.BlockSpec((tm, d), lambda i: (i, 0))
    return pl.pallas_call(
        _rwkv_out_kernel,
        out_shape=jax.ShapeDtypeStruct((m, d), F32),
        grid=(m // tm,),
        in_specs=[tile, tile, tile,
                  pl.BlockSpec((1, 1, d), lambda i: (i // per_seq, 0, 0)),
                  pl.BlockSpec((d, d), lambda i: (0, 0))],
        out_specs=tile,
        compiler_params=_params("parallel"),
        name="rwkv_out",
    )(y, g, x, gate, wo)


def kernel(x, c, norm_g, final_g, ada_w, ada_b, conv_w_in, conv_w, conv_w_out, rw_mu, rw_w_rkv, rw_w_o, rw_w0, rw_w1, rw_w2, rw_a0, rw_a1, rw_a2, rw_g1, rw_g2, rw_k_k, rw_k_a, rw_r_k, rw_ln_w, rw_ln_b, rw_v0, rw_v1, rw_v2, mlp_w1, mlp_w2):
    bsz, seq, d = x.shape
    depth = ada_w.shape[0]
    assert d % GROUP_LANES == 0 and seq % WKV_CHUNK == 0
    bf = lambda w: w.astype(BF16)

    mod = _ada_mod(c, ada_w, ada_b).reshape(depth, bsz, 6, 1, d)
    fg = final_g.reshape(1, d)
    xf = x.reshape(bsz * seq, d)
    v_first = None
    for i in range(depth):
        sh1, sc1, gt1, sh2, sc2, gt2 = (mod[i, :, n] for n in range(6))
        a1 = norm_g[i, 0] * (1.0 + sc1)
        a2 = norm_g[i, 1] * (1.0 + sc2)
        j = i // 2
        if i % 2 == 0:
            xf = _conv_mixer(xf, a1, sh1, gt1, bf(conv_w_in[j]), conv_w[j], bf(conv_w_out[j]), seq)
        else:
            vec = jnp.stack([rw_w0[j], rw_a0[j], rw_k_k[j], rw_k_a[j],
                             rw_v0[j - 1] if v_first is not None else jnp.zeros((d,), F32)])
            weights = [bf(rw_w_rkv[j, 0]), bf(rw_w_rkv[j, 1]), bf(rw_w_rkv[j, 2]),
                       bf(rw_w1[j]), bf(rw_w2[j]), bf(rw_a1[j]), bf(rw_a2[j]),
                       bf(rw_g1[j]), bf(rw_g2[j])]
            vres_w = None if v_first is None else [bf(rw_v1[j - 1]), bf(rw_v2[j - 1])]
            r, k, v, lw, kk, kka, g = _rwkv_in(xf, a1, sh1, rw_mu[j], vec, weights,
                                               v_first, vres_w, seq)
            if v_first is None:
                v_first = v
            y = _wkv(r, k, v, lw, kk, kka, rw_r_k[j].reshape(1, d), rw_ln_w[j].reshape(1, d),
                     rw_ln_b[j].reshape(1, d), bsz, seq)
            xf = _rwkv_out(y, g, xf, gt1, bf(rw_w_o[j]), seq)
        xf = _mlp(xf, a2, sh2, gt2, fg, bf(mlp_w1[i]), bf(mlp_w2[i]), seq, i == depth - 1)
    return xf.reshape(bsz, seq, d)
```

```python
import functools

import jax
import jax.numpy as jnp
from jax import lax
from jax.experimental import pallas as pl
from jax.experimental.pallas import tpu as pltpu

F32 = jnp.float32
BF16 = jnp.bfloat16

HEAD_SIZE = 64
WKV_CHUNK = 64
GROUP_LANES = 256
VREG_LANES = 128
WKV_PAIR = 2
NORM_EPS = 1e-6
GN_EPS = 64e-5
VMEM_LIMIT_BYTES = 56 * 1024 * 1024

MLP_TM = 1024
MLP_TF = 1024
MLP_ROW_BLOCKS = 4
CONV_TM = 512
RWKV_IN_TM = 256
WKV_TT = 512
RWKV_OUT_TM = 512


def _params(*semantics):
    return pltpu.CompilerParams(dimension_semantics=semantics,
                                vmem_limit_bytes=VMEM_LIMIT_BYTES)


def _dot(a, b):
    return jnp.dot(a, b, preferred_element_type=F32)


def _sigmoid(x):
    return 1.0 / (1.0 + jnp.exp(-x))


def _norm_mod(x, a, s):
    ms = jnp.mean(x * x, axis=-1, keepdims=True)
    return x * lax.rsqrt(ms + NORM_EPS) * a + s


def _head_mask():
    shape = (GROUP_LANES, GROUP_LANES)
    return (lax.broadcasted_iota(jnp.int32, shape, 0) // HEAD_SIZE
            == lax.broadcasted_iota(jnp.int32, shape, 1) // HEAD_SIZE)


def _head_ones():
    return jnp.where(_head_mask(), 1.0, 0.0).astype(BF16)


def _shift_rows(x, prev, n):
    row = lax.broadcasted_iota(jnp.int32, (x.shape[0], 1), 0)
    out = pltpu.roll(x, n, 0)
    for j in range(n):
        out = jnp.where(row == j, prev[8 - n + j:8 - n + j + 1], out)
    return out


def _ada_kernel(c_ref, w_ref, b_ref, o_ref):
    c = c_ref[...]
    ca = (c * _sigmoid(c)).astype(BF16)
    o_ref[0] = _dot(ca, w_ref[0].astype(BF16)) + b_ref[0]


def _ada_mod(c, ada_w, ada_b):
    depth, d, n = ada_w.shape
    bsz = c.shape[0]
    rows = 16
    cp = jnp.zeros((rows, d), F32).at[:bsz].set(c)
    tn = min(n, 2048)
    out = pl.pallas_call(
        _ada_kernel,
        out_shape=jax.ShapeDtypeStruct((depth, rows, n), F32),
        grid=(depth, n // tn),
        in_specs=[pl.BlockSpec((rows, d), lambda i, j: (0, 0)),
                  pl.BlockSpec((1, d, tn), lambda i, j: (i, 0, j)),
                  pl.BlockSpec((1, 1, tn), lambda i, j: (i, 0, j))],
        out_specs=pl.BlockSpec((1, rows, tn), lambda i, j: (i, 0, j)),
        compiler_params=_params("parallel", "parallel"),
        name="ada_mod",
    )(cp, ada_w, ada_b.reshape(depth, 1, n))
    return out[:, :bsz]


def _mlp_kernel(x_ref, a_ref, s_ref, g_ref, fg_ref, w1_ref, w2_ref, o_ref, acc_ref, *, final_norm):
    f = pl.program_id(1)

    @pl.when(f == 0)
    def _():
        acc_ref[...] = jnp.zeros_like(acc_ref)

    w1 = w1_ref[0].astype(BF16)
    w2 = w2_ref[0].astype(BF16)
    rows_per_block = x_ref.shape[0] // MLP_ROW_BLOCKS
    for q in range(MLP_ROW_BLOCKS):
        rows = slice(q * rows_per_block, (q + 1) * rows_per_block)
        h = _norm_mod(x_ref[rows], a_ref[0], s_ref[0]).astype(BF16)
        z = jnp.maximum(_dot(h, w1), 0.0)
        acc_ref[rows] += _dot((z * z).astype(BF16), w2)

    @pl.when(f == pl.num_programs(1) - 1)
    def _():
        y = x_ref[...] + g_ref[0] * acc_ref[...]
        if final_norm:
            ms = jnp.mean(y * y, axis=-1, keepdims=True)
            y = y * lax.rsqrt(ms + NORM_EPS) * fg_ref[...]
        o_ref[...] = y


def _mlp(x, a, s, g, fg, w1, w2, layer, seq, final_norm):
    m, d = x.shape
    ff = w1.shape[2]
    tm = min(seq, MLP_TM)
    tf = min(ff, MLP_TF)
    per_seq = seq // tm
    vec = pl.BlockSpec((1, 1, d), lambda i, f: (i // per_seq, 0, 0))
    return pl.pallas_call(
        functools.partial(_mlp_kernel, final_norm=final_norm),
        out_shape=jax.ShapeDtypeStruct((m, d), F32),
        grid=(m // tm, ff // tf),
        in_specs=[pl.BlockSpec((tm, d), lambda i, f: (i, 0)), vec, vec, vec,
                  pl.BlockSpec((1, d), lambda i, f: (0, 0)),
                  pl.BlockSpec((1, d, tf), lambda i, f: (layer, 0, f)),
                  pl.BlockSpec((1, tf, d), lambda i, f: (layer, f, 0))],
        out_specs=pl.BlockSpec((tm, d), lambda i, f: (i, 0)),
        scratch_shapes=[pltpu.VMEM((tm, d), F32)],
        compiler_params=_params("parallel", "arbitrary"),
        name="mlp",
    )(x, a, s, g, fg, w1, w2)


def _conv_kernel(x_ref, a_ref, s_ref, g_ref, win_ref, cw_ref, wo_ref, o_ref, carry_ref, *, per_seq):
    i = pl.program_id(0)
    tm, d = x_ref.shape

    @pl.when(i % per_seq == 0)
    def _():
        carry_ref[...] = jnp.zeros_like(carry_ref)

    x = x_ref[...]
    h = _norm_mod(x, a_ref[0], s_ref[0]).astype(BF16)
    u = _dot(h, win_ref[:, d:2 * d]) * _dot(h, win_ref[:, 2 * d:])
    prev = carry_ref[...]
    cw = cw_ref[...]
    conv = cw[0:1] * _shift_rows(u, prev, 2) + cw[1:2] * _shift_rows(u, prev, 1) + cw[2:3] * u
    carry_ref[...] = u[tm - 8:tm]
    y = (_dot(h, win_ref[:, :d]) * conv).astype(BF16)
    o_ref[...] = x + g_ref[0] * _dot(y, wo_ref[...])


def _conv_mixer(x, a, s, g, w_in, cw, wo, seq):
    m, d = x.shape
    tm = min(seq, CONV_TM)
    per_seq = seq // tm
    tile = pl.BlockSpec((tm, d), lambda i: (i, 0))
    vec = pl.BlockSpec((1, 1, d), lambda i: (i // per_seq, 0, 0))
    return pl.pallas_call(
        functools.partial(_conv_kernel, per_seq=per_seq),
        out_shape=jax.ShapeDtypeStruct((m, d), F32),
        grid=(m // tm,),
        in_specs=[tile, vec, vec, vec,
                  pl.BlockSpec(w_in.shape, lambda i: (0, 0)),
                  pl.BlockSpec(cw.shape, lambda i: (0, 0)),
                  pl.BlockSpec((d, d), lambda i: (0, 0))],
        out_specs=tile,
        scratch_shapes=[pltpu.VMEM((8, d), F32)],
        compiler_params=_params("arbitrary"),
        name="conv_mixer",
    )(x, a, s, g, w_in, cw, wo)


def _rwkv_in_kernel(*refs, per_seq, use_vres):
    if use_vres:
        (x_ref, a_ref, s_ref, mu_ref, vec_ref, wr_ref, wk_ref, wv_ref, w1_ref, w2_ref,
         a1_ref, a2_ref, g1_ref, g2_ref, vf_ref, v1_ref, v2_ref,
         r_ref, k_ref, v_ref, lw_ref, kk_ref, kka_ref, g_ref, carry_ref) = refs
    else:
        (x_ref, a_ref, s_ref, mu_ref, vec_ref, wr_ref, wk_ref, wv_ref, w1_ref, w2_ref,
         a1_ref, a2_ref, g1_ref, g2_ref,
         r_ref, k_ref, v_ref, lw_ref, kk_ref, kka_ref, g_ref, carry_ref) = refs
    i = pl.program_id(0)
    tm, d = x_ref.shape

    @pl.when(i % per_seq == 0)
    def _():
        carry_ref[...] = jnp.zeros_like(carry_ref)

    h = _norm_mod(x_ref[...], a_ref[0], s_ref[0])
    xx = _shift_rows(h, carry_ref[...], 1) - h
    carry_ref[...] = h[tm - 8:tm]

    mu = mu_ref[...]
    vec = vec_ref[...]
    def mix(n):
        return (h + xx * mu[n:n + 1]).astype(BF16)
    xr, xw, xk, xv, xa, xg = (mix(n) for n in range(6))

    r = _dot(xr, wr_ref[...])
    k = _dot(xk, wk_ref[...])
    v = _dot(xv, wv_ref[...])
    wl = vec[0:1] + _dot(jnp.tanh(_dot(xw, w1_ref[...])).astype(BF16), w2_ref[...])
    a = _sigmoid(vec[1:2] + _dot(_dot(xa, a1_ref[...]).astype(BF16), a2_ref[...]))
    g = _dot(_sigmoid(_dot(xg, g1_ref[...])).astype(BF16), g2_ref[...])
    if use_vres:
        mix_v = _sigmoid(vec[4:5] + _dot(_dot(xv, v1_ref[...]).astype(BF16), v2_ref[...]))
        v = v + (vf_ref[...] - v) * mix_v

    lw_ref[...] = -jnp.exp(-0.5) * _sigmoid(wl)

    kk = k * vec[2:3]
    sq = (kk * kk).astype(BF16)
    ones = _head_ones()
    ss = jnp.concatenate([_dot(sq[:, n * GROUP_LANES:(n + 1) * GROUP_LANES], ones)
                          for n in range(d // GROUP_LANES)], axis=1)
    kk = kk / jnp.maximum(jnp.sqrt(ss), 1e-12)
    r_ref[...] = r
    k_ref[...] = k * (1.0 + (a - 1.0) * vec[3:4])
    v_ref[...] = v
    kk_ref[...] = kk
    kka_ref[...] = kk * a
    g_ref[...] = g


def _rwkv_in(x, a, s, mu, vec, weights, vfirst, vres_w, seq):
    m, d = x.shape
    tm = min(seq, RWKV_IN_TM)
    per_seq = seq // tm
    use_vres = vfirst is not None
    tile = pl.BlockSpec((tm, d), lambda i: (i, 0))
    mvec = pl.BlockSpec((1, 1, d), lambda i: (i // per_seq, 0, 0))

    def full(arr):
        return pl.BlockSpec(arr.shape, lambda i: (0,) * arr.ndim)

    args = [x, a, s, mu, vec] + list(weights)
    specs = [tile, mvec, mvec, full(mu), full(vec)] + [full(w) for w in weights]
    if use_vres:
        args += [vfirst] + list(vres_w)
        specs += [tile] + [full(w) for w in vres_w]
    out = jax.ShapeDtypeStruct((m, d), F32)
    return pl.pallas_call(
        functools.partial(_rwkv_in_kernel, per_seq=per_seq, use_vres=use_vres),
        out_shape=(out,) * 7,
        grid=(m // tm,),
        in_specs=specs,
        out_specs=(tile,) * 7,
        scratch_shapes=[pltpu.VMEM((8, d), F32)],
        compiler_params=_params("arbitrary"),
        name="rwkv_in",
    )(*args)


def _block_diag(x):
    n_rows, width = x.shape
    xb = x.astype(BF16)
    lane_head = lax.broadcasted_iota(jnp.int32, (n_rows, VREG_LANES), 1) // HEAD_SIZE
    zero = jnp.zeros((n_rows, VREG_LANES), BF16)
    n_tiles = width // VREG_LANES
    blocks = []
    for h in range(width // HEAD_SIZE):
        tile, sub = divmod(h * HEAD_SIZE, VREG_LANES)
        piece = jnp.where(lane_head == sub // HEAD_SIZE,
                          xb[:, tile * VREG_LANES:(tile + 1) * VREG_LANES], zero)
        blocks.append(jnp.concatenate([piece if n == tile else zero for n in range(n_tiles)], axis=1))
    return jnp.concatenate(blocks, axis=0)


def _block_diag_t(x):
    n_rows, width = x.shape
    reps = VREG_LANES // n_rows
    shape = (VREG_LANES, VREG_LANES)
    keep = (lax.broadcasted_iota(jnp.int32, shape, 0) // HEAD_SIZE
            == lax.broadcasted_iota(jnp.int32, shape, 1) // n_rows)
    n_tiles = width // VREG_LANES
    zero = jnp.zeros(shape, BF16)
    blocks = []
    for tile in range(n_tiles):
        xt = x[:, tile * VREG_LANES:(tile + 1) * VREG_LANES]
        sq = jnp.where(keep, jnp.concatenate([xt] * reps, axis=0).T, 0.0).astype(BF16)
        blocks.append(jnp.concatenate([sq if n == tile else zero for n in range(n_tiles)], axis=1))
    return jnp.concatenate(blocks, axis=0)


def _wkv_kernel(r_ref, k_ref, v_ref, lw_ref, kk_ref, kka_ref, rk_ref, lnw_ref, lnb_ref,
                o_ref, state_ref, wr_s, u0_s, arb_s, ark_s, bkt_s, gm_s):
    tt, d = r_ref.shape
    c_len, gl = WKV_CHUNK, GROUP_LANES
    n_pairs = tt // (c_len * WKV_PAIR)
    n_groups = d // gl
    n_sq = (c_len - 1).bit_length() - 1

    @pl.when(pl.program_id(1) == 0)
    def _():
        state_ref[...] = jnp.zeros_like(state_ref)

    row = lax.broadcasted_iota(jnp.int32, (c_len, gl), 0)
    col = lax.broadcasted_iota(jnp.int32, (c_len, gl), 1) % c_len
    strict = col < row
    incl = col <= row
    eye = jnp.where(col == row, 1.0, 0.0)
    bd_mask = _head_mask()
    tri = jnp.where(lax.broadcasted_iota(jnp.int32, (c_len, c_len), 1)
                    <= lax.broadcasted_iota(jnp.int32, (c_len, c_len), 0), 1.0, 0.0).astype(BF16)
    tri3 = jnp.concatenate([tri, tri, tri], axis=1)
    head_ones = jnp.where(bd_mask, 1.0, 0.0).astype(BF16)
    inv_n = 1.0 / HEAD_SIZE

    def bf(x):
        return x.astype(BF16)

    def lanes(gi):
        return slice(gi * gl, (gi + 1) * gl)

    def chunk_rows(c):
        start = c * c_len
        return pl.ds(start if isinstance(start, int) else pl.multiple_of(start, c_len), c_len)

    def prepare(p):
        items = [(p * WKV_PAIR + ci, gi) for ci in range(WKV_PAIR) for gi in range(n_groups)]

        cums = []
        for c, gi in items:
            lw = lw_ref[chunk_rows(c), lanes(gi)]
            p0 = lw.astype(BF16)
            rem = lw - p0.astype(F32)
            p1 = rem.astype(BF16)
            p2 = (rem - p1.astype(F32)).astype(BF16)
            cums.append(_dot(tri3, jnp.concatenate([p0, p1, p2], axis=0)))
        yield

        at, lhs, bt, kt = [], [], [], []
        for (c, gi), cum in zip(items, cums):
            rs, ls, slot = chunk_rows(c), lanes(gi), c * n_groups + gi
            c_last = cum[c_len - 1:c_len]
            kka = kka_ref[rs, ls]
            kx = k_ref[rs, ls]
            e_inv = jnp.exp(-cum)
            e_tail = jnp.exp(c_last - cum)
            a_t = -kk_ref[rs, ls] * jnp.exp(cum - lw_ref[rs, ls])
            r_t = r_ref[rs, ls] * jnp.exp(cum)
            at.append(bf(a_t))
            lhs.append(bf(jnp.concatenate([a_t, r_t], axis=0)))
            bt.append(kka * e_inv)
            kt.append(kx * e_inv)
            bkt_s[slot] = bf(jnp.concatenate([kka * e_tail, kx * e_tail], axis=0).T)
            gm_s[slot] = jnp.broadcast_to(jnp.exp(c_last), (VREG_LANES, gl)).T

        sb = [_dot(l, _block_diag_t(b)) for l, b in zip(lhs, bt)]
        sk = [_dot(l, _block_diag_t(k)) for l, k in zip(lhs, kt)]
        for (c, gi), b, k in zip(items, sb, sk):
            arb_s[c * n_groups + gi] = bf(jnp.where(incl, b[c_len:], 0.0))
            ark_s[c * n_groups + gi] = bf(jnp.where(incl, k[c_len:], 0.0))
        yield

        xs = [jnp.where(strict, b[:c_len], 0.0) for b in sb]
        ts = [eye + x for x in xs]
        xs = [_dot(bf(x), _block_diag(x)) for x in xs]
        yield
        for _ in range(n_sq - 1):
            res = [_dot(bf(jnp.concatenate([x, t], axis=0)), _block_diag(x)) for x, t in zip(xs, ts)]
            xs = [r[:c_len] for r in res]
            ts = [t + r[c_len:] for t, r in zip(ts, res)]
            yield
        ts = [bf(t + _dot(bf(t), _block_diag(x))) for x, t in zip(xs, ts)]
        yield

        akv = [_dot(bf(jnp.where(strict, k[:c_len], 0.0)), _block_diag(v_ref[chunk_rows(c), lanes(gi)]))
               for (c, gi), k in zip(items, sk)]
        ws = [_dot(t, _block_diag(a)) for t, a in zip(ts, at)]
        u0 = [_dot(t, _block_diag(a)) for t, a in zip(ts, akv)]
        for (c, gi), w, u, l in zip(items, ws, u0, lhs):
            wr_s[c * n_groups + gi] = jnp.concatenate([bf(w), l[c_len:]], axis=0)
            u0_s[c * n_groups + gi] = u

    def advance_chunk(c):
        rs = chunk_rows(c)
        slots = [c * n_groups + gi for gi in range(n_groups)]
        st = [state_ref[gi] for gi in range(n_groups)]
        vs = [v_ref[rs, lanes(gi)] for gi in range(n_groups)]
        ps = [_dot(wr_s[s], bf(x)) for s, x in zip(slots, st)]
        yield
        us = [u0_s[s] + p[:c_len] for s, p in zip(slots, ps)]
        upd = [_dot(bkt_s[s], bf(jnp.concatenate([u, v], axis=0))) for s, u, v in zip(slots, us, vs)]
        for gi in range(n_groups):
            gm = gm_s[slots[gi]]
            state_ref[gi] = (st[gi] * jnp.concatenate([gm] * (gl // VREG_LANES), axis=1)
                             + jnp.where(bd_mask, upd[gi], 0.0))
        yield
        ys = [p[c_len:] + _dot(arb_s[s], _block_diag(u)) + _dot(ark_s[s], _block_diag(v))
              for s, p, u, v in zip(slots, ps, us, vs)]
        rows_in = []
        for gi, y in enumerate(ys):
            rows_in += [y, r_ref[rs, lanes(gi)] * k_ref[rs, lanes(gi)] * rk_ref[:, lanes(gi)]]
        sums = _dot(bf(jnp.concatenate(rows_in, axis=0)), head_ones)
        yield
        devs = [y - sums[2 * gi * c_len:(2 * gi + 1) * c_len] * inv_n for gi, y in enumerate(ys)]
        var = _dot(bf(jnp.concatenate([dv * dv for dv in devs], axis=0)), head_ones) * inv_n
        for gi in range(n_groups):
            ls = lanes(gi)
            o_ref[rs, ls] = (devs[gi] * lax.rsqrt(var[gi * c_len:(gi + 1) * c_len] + GN_EPS)
                             * lnw_ref[:, ls] + lnb_ref[:, ls]
                             + sums[(2 * gi + 1) * c_len:(2 * gi + 2) * c_len] * vs[gi])

    def run(stages, order):
        for name in order:
            next(stages[name], None)

    assert WKV_PAIR == 2
    run({"p": prepare(0)}, "p" * (n_sq + 4))

    def body(p, carry):
        stages = {"p": prepare(p), "a": advance_chunk((p - 1) * WKV_PAIR),
                  "b": advance_chunk((p - 1) * WKV_PAIR + 1)}
        run(stages, "apapbppbpapapbpbp" + "p" * (n_sq - 5))
        return carry

    lax.fori_loop(1, n_pairs, body, 0)
    last = (n_pairs - 1) * WKV_PAIR
    run({"a": advance_chunk(last), "b": advance_chunk(last + 1)}, "aabababb")


def _wkv(r, k, v, lw, kk, kka, rk, lnw, lnb, bsz, seq):
    m, d = r.shape
    tt = min(seq, WKV_TT)
    assert tt % (WKV_CHUNK * WKV_PAIR) == 0
    per_seq = seq // tt
    n_groups = d // GROUP_LANES
    n_slots = tt // WKV_CHUNK * n_groups
    tile = pl.BlockSpec((tt, d), lambda b, t: (b * per_seq + t, 0))
    vec = pl.BlockSpec((1, d), lambda b, t: (0, 0))
    return pl.pallas_call(
        _wkv_kernel,
        out_shape=jax.ShapeDtypeStruct((m, d), F32),
        grid=(bsz, per_seq),
        in_specs=[tile] * 6 + [vec] * 3,
        out_specs=tile,
        scratch_shapes=[pltpu.VMEM((n_groups, GROUP_LANES, GROUP_LANES), F32),
                        pltpu.VMEM((n_slots, 2 * WKV_CHUNK, GROUP_LANES), BF16),
                        pltpu.VMEM((n_slots, WKV_CHUNK, GROUP_LANES), F32),
                        pltpu.VMEM((n_slots, WKV_CHUNK, GROUP_LANES), BF16),
                        pltpu.VMEM((n_slots, WKV_CHUNK, GROUP_LANES), BF16),
                        pltpu.VMEM((n_slots, GROUP_LANES, 2 * WKV_CHUNK), BF16),
                        pltpu.VMEM((n_slots, GROUP_LANES, VREG_LANES), F32)],
        compiler_params=_params("parallel", "arbitrary"),
        name="wkv",
    )(r, k, v, lw, kk, kka, rk, lnw, lnb)


def _rwkv_out_kernel(y_ref, g_ref, x_ref, gate_ref, wo_ref, o_ref):
    z = (y_ref[...] * g_ref[...]).astype(BF16)
    o_ref[...] = x_ref[...] + gate_ref[0] * _dot(z, wo_ref[...])


def _rwkv_out(y, g, x, gate, wo, seq):
    m, d = x.shape
    tm = min(seq, RWKV_OUT_TM)
    per_seq = seq // tm
    tile = pl.BlockSpec((tm, d), lambda i: (i, 0))
    return pl.pallas_call(
        _rwkv_out_kernel,
        out_shape=jax.ShapeDtypeStruct((m, d), F32),
        grid=(m // tm,),
        in_specs=[tile, tile, tile,
                  pl.BlockSpec((1, 1, d), lambda i: (i // per_seq, 0, 0)),
                  pl.BlockSpec((d, d), lambda i: (0, 0))],
        out_specs=tile,
        compiler_params=_params("parallel"),
        name="rwkv_out",
    )(y, g, x, gate, wo)


def kernel(x, c, norm_g, final_g, ada_w, ada_b, conv_w_in, conv_w, conv_w_out, rw_mu, rw_w_rkv, rw_w_o, rw_w0, rw_w1, rw_w2, rw_a0, rw_a1, rw_a2, rw_g1, rw_g2, rw_k_k, rw_k_a, rw_r_k, rw_ln_w, rw_ln_b, rw_v0, rw_v1, rw_v2, mlp_w1, mlp_w2):
    bsz, seq, d = x.shape
    depth = ada_w.shape[0]
    assert d % GROUP_LANES == 0 and seq % WKV_CHUNK == 0
    bf = lambda w: w.astype(BF16)

    mod = _ada_mod(c, ada_w, ada_b).reshape(depth, bsz, 6, 1, d)
    fg = final_g.reshape(1, d)
    xf = x.reshape(bsz * seq, d)
    v_first = None
    for i in range(depth):
        sh1, sc1, gt1, sh2, sc2, gt2 = (mod[i, :, n] for n in range(6))
        a1 = norm_g[i, 0] * (1.0 + sc1)
        a2 = norm_g[i, 1] * (1.0 + sc2)
        j = i // 2
        if i % 2 == 0:
            xf = _conv_mixer(xf, a1, sh1, gt1, bf(conv_w_in[j]), conv_w[j], bf(conv_w_out[j]), seq)
        else:
            vec = jnp.stack([rw_w0[j], rw_a0[j], rw_k_k[j], rw_k_a[j],
                             rw_v0[j - 1] if v_first is not None else jnp.zeros((d,), F32)])
            weights = [bf(rw_w_rkv[j, 0]), bf(rw_w_rkv[j, 1]), bf(rw_w_rkv[j, 2]),
                       bf(rw_w1[j]), bf(rw_w2[j]), bf(rw_a1[j]), bf(rw_a2[j]),
                       bf(rw_g1[j]), bf(rw_g2[j])]
            vres_w = None if v_first is None else [bf(rw_v1[j - 1]), bf(rw_v2[j - 1])]
            r, k, v, lw, kk, kka, g = _rwkv_in(xf, a1, sh1, rw_mu[j], vec, weights,
                                               v_first, vres_w, seq)
            if v_first is None:
                v_first = v
            y = _wkv(r, k, v, lw, kk, kka, rw_r_k[j].reshape(1, d), rw_ln_w[j].reshape(1, d),
                     rw_ln_b[j].reshape(1, d), bsz, seq)
            xf = _rwkv_out(y, g, xf, gt1, bf(rw_w_o[j]), seq)
        xf = _mlp(xf, a2, sh2, gt2, fg, mlp_w1, mlp_w2, i, seq, i == depth - 1)
    return xf.reshape(bsz, seq, d)
```

```python
import functools

import jax
import jax.numpy as jnp
from jax import lax
from jax.experimental import pallas as pl
from jax.experimental.pallas import tpu as pltpu

F32 = jnp.float32
BF16 = jnp.bfloat16

HEAD_SIZE = 64
WKV_CHUNK = 64
GROUP_LANES = 256
VREG_LANES = 128
WKV_PAIR = 2
NORM_EPS = 1e-6
GN_EPS = 64e-5
VMEM_LIMIT_BYTES = 56 * 1024 * 1024

MLP_TM = 1024
MLP_TF = 1024
MLP_ROW_BLOCKS = 4
CONV_TM = 512
RWKV_IN_TM = 512
RWKV_IN_ROW_BLOCKS = 2
WKV_TT = 512
WKV_OUT_ROWS = 256


def _params(*semantics):
    return pltpu.CompilerParams(dimension_semantics=semantics,
                                vmem_limit_bytes=VMEM_LIMIT_BYTES)


def _dot(a, b):
    return jnp.dot(a, b, preferred_element_type=F32)


def _sigmoid(x):
    return 1.0 / (1.0 + jnp.exp(-x))


def _norm_mod(x, a, s):
    ms = jnp.mean(x * x, axis=-1, keepdims=True)
    return x * lax.rsqrt(ms + NORM_EPS) * a + s


def _head_mask():
    shape = (GROUP_LANES, GROUP_LANES)
    return (lax.broadcasted_iota(jnp.int32, shape, 0) // HEAD_SIZE
            == lax.broadcasted_iota(jnp.int32, shape, 1) // HEAD_SIZE)


def _head_ones():
    return jnp.where(_head_mask(), 1.0, 0.0).astype(BF16)


def _shift_rows(x, prev, n):
    row = lax.broadcasted_iota(jnp.int32, (x.shape[0], 1), 0)
    out = pltpu.roll(x, n, 0)
    for j in range(n):
        out = jnp.where(row == j, prev[8 - n + j:8 - n + j + 1], out)
    return out


def _ada_kernel(c_ref, w_ref, b_ref, o_ref):
    c = c_ref[...]
    ca = (c * _sigmoid(c)).astype(BF16)
    o_ref[0] = _dot(ca, w_ref[0].astype(BF16)) + b_ref[0]


def _ada_mod(c, ada_w, ada_b):
    depth, d, n = ada_w.shape
    bsz = c.shape[0]
    rows = 16
    cp = jnp.zeros((rows, d), F32).at[:bsz].set(c)
    tn = min(n, 2048)
    out = pl.pallas_call(
        _ada_kernel,
        out_shape=jax.ShapeDtypeStruct((depth, rows, n), F32),
        grid=(depth, n // tn),
        in_specs=[pl.BlockSpec((rows, d), lambda i, j: (0, 0)),
                  pl.BlockSpec((1, d, tn), lambda i, j: (i, 0, j)),
                  pl.BlockSpec((1, 1, tn), lambda i, j: (i, 0, j))],
        out_specs=pl.BlockSpec((1, rows, tn), lambda i, j: (i, 0, j)),
        compiler_params=_params("parallel", "parallel"),
        name="ada_mod",
    )(cp, ada_w, ada_b.reshape(depth, 1, n))
    return out[:, :bsz]


def _mlp_kernel(x_ref, a_ref, s_ref, g_ref, fg_ref, w1_ref, w2_ref, o_ref, acc_ref, *, final_norm):
    f = pl.program_id(1)
    n_f = pl.num_programs(1)

    def step(first, last):
        w1 = w1_ref[0].astype(BF16)
        w2 = w2_ref[0].astype(BF16)
        rows_per_block = x_ref.shape[0] // MLP_ROW_BLOCKS
        for q in range(MLP_ROW_BLOCKS):
            rows = slice(q * rows_per_block, (q + 1) * rows_per_block)
            x = x_ref[rows]
            h = _norm_mod(x, a_ref[0], s_ref[0]).astype(BF16)
            z = jnp.maximum(_dot(h, w1), 0.0)
            part = _dot((z * z).astype(BF16), w2)
            acc = part if first else acc_ref[rows] + part
            if not last:
                acc_ref[rows] = acc
                continue
            y = x + g_ref[0] * acc
            if final_norm:
                ms = jnp.mean(y * y, axis=-1, keepdims=True)
                y = y * lax.rsqrt(ms + NORM_EPS) * fg_ref[...]
            o_ref[rows] = y

    pl.when(f == 0)(functools.partial(step, True, False))
    pl.when(jnp.logical_and(f > 0, f < n_f - 1))(functools.partial(step, False, False))
    pl.when(f == n_f - 1)(functools.partial(step, False, True))


def _mlp(x, a, s, g, fg, w1, w2, layer, seq, final_norm):
    m, d = x.shape
    ff = w1.shape[2]
    tm = min(seq, MLP_TM)
    tf = min(ff, MLP_TF)
    assert ff // tf >= 2
    per_seq = seq // tm
    vec = pl.BlockSpec((1, 1, d), lambda i, f: (i // per_seq, 0, 0))
    return pl.pallas_call(
        functools.partial(_mlp_kernel, final_norm=final_norm),
        out_shape=jax.ShapeDtypeStruct((m, d), F32),
        grid=(m // tm, ff // tf),
        in_specs=[pl.BlockSpec((tm, d), lambda i, f: (i, 0)), vec, vec, vec,
                  pl.BlockSpec((1, d), lambda i, f: (0, 0)),
                  pl.BlockSpec((1, d, tf), lambda i, f: (layer, 0, f)),
                  pl.BlockSpec((1, tf, d), lambda i, f: (layer, f, 0))],
        out_specs=pl.BlockSpec((tm, d), lambda i, f: (i, 0)),
        scratch_shapes=[pltpu.VMEM((tm, d), F32)],
        compiler_params=_params("parallel", "arbitrary"),
        name="mlp",
    )(x, a, s, g, fg, w1, w2)


def _conv_kernel(x_ref, a_ref, s_ref, g_ref, win_ref, cw_ref, wo_ref, o_ref, carry_ref, *, per_seq):
    i = pl.program_id(0)
    tm, d = x_ref.shape

    @pl.when(i % per_seq == 0)
    def _():
        carry_ref[...] = jnp.zeros_like(carry_ref)

    x = x_ref[...]
    h = _norm_mod(x, a_ref[0], s_ref[0]).astype(BF16)
    u = _dot(h, win_ref[:, d:2 * d]) * _dot(h, win_ref[:, 2 * d:])
    prev = carry_ref[...]
    cw = cw_ref[...]
    conv = cw[0:1] * _shift_rows(u, prev, 2) + cw[1:2] * _shift_rows(u, prev, 1) + cw[2:3] * u
    carry_ref[...] = u[tm - 8:tm]
    y = (_dot(h, win_ref[:, :d]) * conv).astype(BF16)
    o_ref[...] = x + g_ref[0] * _dot(y, wo_ref[...])


def _conv_mixer(x, a, s, g, w_in, cw, wo, seq):
    m, d = x.shape
    tm = min(seq, CONV_TM)
    per_seq = seq // tm
    tile = pl.BlockSpec((tm, d), lambda i: (i, 0))
    vec = pl.BlockSpec((1, 1, d), lambda i: (i // per_seq, 0, 0))
    return pl.pallas_call(
        functools.partial(_conv_kernel, per_seq=per_seq),
        out_shape=jax.ShapeDtypeStruct((m, d), F32),
        grid=(m // tm,),
        in_specs=[tile, vec, vec, vec,
                  pl.BlockSpec(w_in.shape, lambda i: (0, 0)),
                  pl.BlockSpec(cw.shape, lambda i: (0, 0)),
                  pl.BlockSpec((d, d), lambda i: (0, 0))],
        out_specs=tile,
        scratch_shapes=[pltpu.VMEM((8, d), F32)],
        compiler_params=_params("arbitrary"),
        name="conv_mixer",
    )(x, a, s, g, w_in, cw, wo)


def _rwkv_in_kernel(*refs, per_seq, use_vres):
    if use_vres:
        (x_ref, a_ref, s_ref, mu_ref, vec_ref, wr_ref, wk_ref, wv_ref, w1_ref, w2_ref,
         a1_ref, a2_ref, g1_ref, g2_ref, vf_ref, v1_ref, v2_ref,
         r_ref, k_ref, v_ref, lw_ref, kk_ref, kka_ref, g_ref, carry_ref) = refs
    else:
        (x_ref, a_ref, s_ref, mu_ref, vec_ref, wr_ref, wk_ref, wv_ref, w1_ref, w2_ref,
         a1_ref, a2_ref, g1_ref, g2_ref,
         r_ref, k_ref, v_ref, lw_ref, kk_ref, kka_ref, g_ref, carry_ref) = refs
    i = pl.program_id(0)
    tm, d = x_ref.shape

    @pl.when(i % per_seq == 0)
    def _():
        carry_ref[...] = jnp.zeros_like(carry_ref)

    mu = mu_ref[...]
    vec = vec_ref[...]
    ones = _head_ones()
    prev = carry_ref[...]

    block = tm // RWKV_IN_ROW_BLOCKS
    for q in range(RWKV_IN_ROW_BLOCKS):
        rows = slice(q * block, (q + 1) * block)
        h = _norm_mod(x_ref[rows], a_ref[0], s_ref[0])
        xx = _shift_rows(h, prev, 1) - h
        prev = h[block - 8:block]

        def mix(n):
            return (h + xx * mu[n:n + 1]).astype(BF16)

        r_ref[rows] = _dot(mix(0), wr_ref[...])

        wl = vec[0:1] + _dot(jnp.tanh(_dot(mix(1), w1_ref[...])).astype(BF16), w2_ref[...])
        lw_ref[rows] = -jnp.exp(-0.5) * _sigmoid(wl)

        g_ref[rows] = _dot(_sigmoid(_dot(mix(5), g1_ref[...])).astype(BF16), g2_ref[...])

        xv = mix(3)
        v = _dot(xv, wv_ref[...])
        if use_vres:
            mix_v = _sigmoid(vec[4:5] + _dot(_dot(xv, v1_ref[...]).astype(BF16), v2_ref[...]))
            v = v + (vf_ref[rows] - v) * mix_v
        v_ref[rows] = v

        a = _sigmoid(vec[1:2] + _dot(_dot(mix(4), a1_ref[...]).astype(BF16), a2_ref[...]))
        k = _dot(mix(2), wk_ref[...])
        k_ref[rows] = k * (1.0 + (a - 1.0) * vec[3:4])
        kk = k * vec[2:3]
        sq = (kk * kk).astype(BF16)
        ss = jnp.concatenate([_dot(sq[:, n * GROUP_LANES:(n + 1) * GROUP_LANES], ones)
                              for n in range(d // GROUP_LANES)], axis=1)
        kk = kk / jnp.maximum(jnp.sqrt(ss), 1e-12)
        kk_ref[rows] = kk
        kka_ref[rows] = kk * a
    carry_ref[...] = prev


def _rwkv_in(x, a, s, mu, vec, weights, vfirst, vres_w, seq):
    m, d = x.shape
    tm = min(seq, RWKV_IN_TM)
    per_seq = seq // tm
    use_vres = vfirst is not None
    tile = pl.BlockSpec((tm, d), lambda i: (i, 0))
    mvec = pl.BlockSpec((1, 1, d), lambda i: (i // per_seq, 0, 0))

    def full(arr):
        return pl.BlockSpec(arr.shape, lambda i: (0,) * arr.ndim, pipeline_mode=pl.Buffered(1))

    args = [x, a, s, mu, vec] + list(weights)
    specs = [tile, mvec, mvec, full(mu), full(vec)] + [full(w) for w in weights]
    if use_vres:
        args += [vfirst] + list(vres_w)
        specs += [tile] + [full(w) for w in vres_w]
    out = jax.ShapeDtypeStruct((m, d), F32)
    return pl.pallas_call(
        functools.partial(_rwkv_in_kernel, per_seq=per_seq, use_vres=use_vres),
        out_shape=(out,) * 7,
        grid=(m // tm,),
        in_specs=specs,
        out_specs=(tile,) * 7,
        scratch_shapes=[pltpu.VMEM((8, d), F32)],
        compiler_params=_params("arbitrary"),
        name="rwkv_in",
    )(*args)


def _block_diag(x):
    n_rows, width = x.shape
    xb = x.astype(BF16)
    lane_head = lax.broadcasted_iota(jnp.int32, (n_rows, VREG_LANES), 1) // HEAD_SIZE
    zero = jnp.zeros((n_rows, VREG_LANES), BF16)
    n_tiles = width // VREG_LANES
    blocks = []
    for h in range(width // HEAD_SIZE):
        tile, sub = divmod(h * HEAD_SIZE, VREG_LANES)
        piece = jnp.where(lane_head == sub // HEAD_SIZE,
                          xb[:, tile * VREG_LANES:(tile + 1) * VREG_LANES], zero)
        blocks.append(jnp.concatenate([piece if n == tile else zero for n in range(n_tiles)], axis=1))
    return jnp.concatenate(blocks, axis=0)


def _block_diag_t(x):
    n_rows, width = x.shape
    reps = VREG_LANES // n_rows
    shape = (VREG_LANES, VREG_LANES)
    keep = (lax.broadcasted_iota(jnp.int32, shape, 0) // HEAD_SIZE
            == lax.broadcasted_iota(jnp.int32, shape, 1) // n_rows)
    n_tiles = width // VREG_LANES
    zero = jnp.zeros(shape, BF16)
    blocks = []
    for tile in range(n_tiles):
        xt = x[:, tile * VREG_LANES:(tile + 1) * VREG_LANES]
        sq = jnp.where(keep, jnp.concatenate([xt] * reps, axis=0).T, 0.0).astype(BF16)
        blocks.append(jnp.concatenate([sq if n == tile else zero for n in range(n_tiles)], axis=1))
    return jnp.concatenate(blocks, axis=0)


def _wkv_kernel(r_ref, k_ref, v_ref, lw_ref, kk_ref, kka_ref, rk_ref, lnw_ref, lnb_ref,
                g_ref, x_ref, gate_ref, wo_ref,
                o_ref, state_ref, wr_s, u0_s, arb_s, ark_s, bkt_s, gm_s):
    tt, d = r_ref.shape
    c_len, gl = WKV_CHUNK, GROUP_LANES
    n_pairs = tt // (c_len * WKV_PAIR)
    n_groups = d // gl
    n_sq = (c_len - 1).bit_length() - 1

    @pl.when(pl.program_id(1) == 0)
    def _():
        state_ref[...] = jnp.zeros_like(state_ref)

    row = lax.broadcasted_iota(jnp.int32, (c_len, gl), 0)
    col = lax.broadcasted_iota(jnp.int32, (c_len, gl), 1) % c_len
    strict = col < row
    incl = col <= row
    eye = jnp.where(col == row, 1.0, 0.0)
    bd_mask = _head_mask()
    tri = jnp.where(lax.broadcasted_iota(jnp.int32, (c_len, c_len), 1)
                    <= lax.broadcasted_iota(jnp.int32, (c_len, c_len), 0), 1.0, 0.0).astype(BF16)
    tri3 = jnp.concatenate([tri, tri, tri], axis=1)
    head_ones = jnp.where(bd_mask, 1.0, 0.0).astype(BF16)
    inv_n = 1.0 / HEAD_SIZE

    def bf(x):
        return x.astype(BF16)

    def lanes(gi):
        return slice(gi * gl, (gi + 1) * gl)

    def slot_of(c, gi):
        ring = 2 * WKV_PAIR
        return (c % ring if isinstance(c, int) else lax.rem(c, ring)) * n_groups + gi

    def chunk_rows(c):
        start = c * c_len
        return pl.ds(start if isinstance(start, int) else pl.multiple_of(start, c_len), c_len)

    def prepare(p):
        items = [(p * WKV_PAIR + ci, gi) for ci in range(WKV_PAIR) for gi in range(n_groups)]

        cums = []
        for c, gi in items:
            lw = lw_ref[chunk_rows(c), lanes(gi)]
            p0 = lw.astype(BF16)
            rem = lw - p0.astype(F32)
            p1 = rem.astype(BF16)
            p2 = (rem - p1.astype(F32)).astype(BF16)
            cums.append(_dot(tri3, jnp.concatenate([p0, p1, p2], axis=0)))
        yield

        at, lhs, bt, kt = [], [], [], []
        for (c, gi), cum in zip(items, cums):
            rs, ls, slot = chunk_rows(c), lanes(gi), slot_of(c, gi)
            c_last = cum[c_len - 1:c_len]
            kka = kka_ref[rs, ls]
            kx = k_ref[rs, ls]
            e_inv = jnp.exp(-cum)
            e_tail = jnp.exp(c_last - cum)
            a_t = -kk_ref[rs, ls] * jnp.exp(cum - lw_ref[rs, ls])
            r_t = r_ref[rs, ls] * jnp.exp(cum)
            at.append(bf(a_t))
            lhs.append(bf(jnp.concatenate([a_t, r_t], axis=0)))
            bt.append(kka * e_inv)
            kt.append(kx * e_inv)
            bkt_s[slot] = bf(jnp.concatenate([kka * e_tail, kx * e_tail], axis=0).T)
            gm_s[slot] = jnp.broadcast_to(jnp.exp(c_last), (VREG_LANES, gl)).T

        sb = [_dot(l, _block_diag_t(b)) for l, b in zip(lhs, bt)]
        sk = [_dot(l, _block_diag_t(k)) for l, k in zip(lhs, kt)]
        for (c, gi), b, k in zip(items, sb, sk):
            arb_s[slot_of(c, gi)] = bf(jnp.where(incl, b[c_len:], 0.0))
            ark_s[slot_of(c, gi)] = bf(jnp.where(incl, k[c_len:], 0.0))
        yield

        xs = [jnp.where(strict, b[:c_len], 0.0) for b in sb]
        ts = [eye + x for x in xs]
        xs = [_dot(bf(x), _block_diag(x)) for x in xs]
        yield
        for _ in range(n_sq - 1):
            res = [_dot(bf(jnp.concatenate([x, t], axis=0)), _block_diag(x)) for x, t in zip(xs, ts)]
            xs = [r[:c_len] for r in res]
            ts = [t + r[c_len:] for t, r in zip(ts, res)]
            yield
        ts = [bf(t + _dot(bf(t), _block_diag(x))) for x, t in zip(xs, ts)]
        yield

        akv = [_dot(bf(jnp.where(strict, k[:c_len], 0.0)), _block_diag(v_ref[chunk_rows(c), lanes(gi)]))
               for (c, gi), k in zip(items, sk)]
        ws = [_dot(t, _block_diag(a)) for t, a in zip(ts, at)]
        u0 = [_dot(t, _block_diag(a)) for t, a in zip(ts, akv)]
        for (c, gi), w, u, l in zip(items, ws, u0, lhs):
            wr_s[slot_of(c, gi)] = jnp.concatenate([bf(w), l[c_len:]], axis=0)
            u0_s[slot_of(c, gi)] = u

    def advance_chunk(c):
        rs = chunk_rows(c)
        slots = [slot_of(c, gi) for gi in range(n_groups)]
        st = [state_ref[gi] for gi in range(n_groups)]
        vs = [v_ref[rs, lanes(gi)] for gi in range(n_groups)]
        ps = [_dot(wr_s[s], bf(x)) for s, x in zip(slots, st)]
        yield
        us = [u0_s[s] + p[:c_len] for s, p in zip(slots, ps)]
        upd = [_dot(bkt_s[s], bf(jnp.concatenate([u, v], axis=0))) for s, u, v in zip(slots, us, vs)]
        for gi in range(n_groups):
            gm = gm_s[slots[gi]]
            state_ref[gi] = (st[gi] * jnp.concatenate([gm] * (gl // VREG_LANES), axis=1)
                             + jnp.where(bd_mask, upd[gi], 0.0))
        yield
        ys = [p[c_len:] + _dot(arb_s[s], _block_diag(u)) + _dot(ark_s[s], _block_diag(v))
              for s, p, u, v in zip(slots, ps, us, vs)]
        rows_in = []
        for gi, y in enumerate(ys):
            rows_in += [y, r_ref[rs, lanes(gi)] * k_ref[rs, lanes(gi)] * rk_ref[:, lanes(gi)]]
        sums = _dot(bf(jnp.concatenate(rows_in, axis=0)), head_ones)
        yield
        devs = [y - sums[2 * gi * c_len:(2 * gi + 1) * c_len] * inv_n for gi, y in enumerate(ys)]
        var = _dot(bf(jnp.concatenate([dv * dv for dv in devs], axis=0)), head_ones) * inv_n
        for gi in range(n_groups):
            ls = lanes(gi)
            o_ref[rs, ls] = (devs[gi] * lax.rsqrt(var[gi * c_len:(gi + 1) * c_len] + GN_EPS)
                             * lnw_ref[:, ls] + lnb_ref[:, ls]
                             + sums[(2 * gi + 1) * c_len:(2 * gi + 2) * c_len] * vs[gi])

    def run(stages, order):
        for name in order:
            next(stages[name], None)

    assert WKV_PAIR == 2
    run({"p": prepare(0)}, "p" * (n_sq + 4))

    def body(p, carry):
        stages = {"p": prepare(p), "a": advance_chunk((p - 1) * WKV_PAIR),
                  "b": advance_chunk((p - 1) * WKV_PAIR + 1)}
        run(stages, "apapbppbpapapbpbp" + "p" * (n_sq - 5))
        return carry

    lax.fori_loop(1, n_pairs, body, 0)
    last = (n_pairs - 1) * WKV_PAIR
    run({"a": advance_chunk(last), "b": advance_chunk(last + 1)}, "aabababb")

    block = min(tt, WKV_OUT_ROWS)
    for q in range(tt // block):
        rows = slice(q * block, (q + 1) * block)
        z = (o_ref[rows] * g_ref[rows]).astype(BF16)
        o_ref[rows] = x_ref[rows] + gate_ref[0] * _dot(z, wo_ref[...])


def _wkv_mixer(r, k, v, lw, kk, kka, rk, lnw, lnb, g, x, gate, wo, bsz, seq):
    m, d = r.shape
    tt = min(seq, WKV_TT)
    assert tt % (WKV_CHUNK * WKV_PAIR) == 0
    per_seq = seq // tt
    n_groups = d // GROUP_LANES
    n_slots = 2 * WKV_PAIR * n_groups
    tile = pl.BlockSpec((tt, d), lambda b, t: (b * per_seq + t, 0))
    vec = pl.BlockSpec((1, d), lambda b, t: (0, 0))
    return pl.pallas_call(
        _wkv_kernel,
        out_shape=jax.ShapeDtypeStruct((m, d), F32),
        grid=(bsz, per_seq),
        in_specs=[tile] * 6 + [vec] * 3 + [tile, tile,
                  pl.BlockSpec((1, 1, d), lambda b, t: (b, 0, 0)),
                  pl.BlockSpec((d, d), lambda b, t: (0, 0), pipeline_mode=pl.Buffered(1))],
        out_specs=tile,
        scratch_shapes=[pltpu.VMEM((n_groups, GROUP_LANES, GROUP_LANES), F32),
                        pltpu.VMEM((n_slots, 2 * WKV_CHUNK, GROUP_LANES), BF16),
                        pltpu.VMEM((n_slots, WKV_CHUNK, GROUP_LANES), F32),
                        pltpu.VMEM((n_slots, WKV_CHUNK, GROUP_LANES), BF16),
                        pltpu.VMEM((n_slots, WKV_CHUNK, GROUP_LANES), BF16),
                        pltpu.VMEM((n_slots, GROUP_LANES, 2 * WKV_CHUNK), BF16),
                        pltpu.VMEM((n_slots, GROUP_LANES, VREG_LANES), F32)],
        compiler_params=_params("parallel", "arbitrary"),
        name="wkv_mixer",
    )(r, k, v, lw, kk, kka, rk, lnw, lnb, g, x, gate, wo)


def kernel(x, c, norm_g, final_g, ada_w, ada_b, conv_w_in, conv_w, conv_w_out, rw_mu, rw_w_rkv, rw_w_o, rw_w0, rw_w1, rw_w2, rw_a0, rw_a1, rw_a2, rw_g1, rw_g2, rw_k_k, rw_k_a, rw_r_k, rw_ln_w, rw_ln_b, rw_v0, rw_v1, rw_v2, mlp_w1, mlp_w2):
    bsz, seq, d = x.shape
    depth = ada_w.shape[0]
    assert d % GROUP_LANES == 0 and seq % WKV_CHUNK == 0
    bf = lambda w: w.astype(BF16)

    mod = _ada_mod(c, ada_w, ada_b).reshape(depth, bsz, 6, 1, d)
    fg = final_g.reshape(1, d)
    xf = x.reshape(bsz * seq, d)
    v_first = None
    for i in range(depth):
        sh1, sc1, gt1, sh2, sc2, gt2 = (mod[i, :, n] for n in range(6))
        a1 = norm_g[i, 0] * (1.0 + sc1)
        a2 = norm_g[i, 1] * (1.0 + sc2)
        j = i // 2
        if i % 2 == 0:
            xf = _conv_mixer(xf, a1, sh1, gt1, bf(conv_w_in[j]), conv_w[j], bf(conv_w_out[j]), seq)
        else:
            vec = jnp.stack([rw_w0[j], rw_a0[j], rw_k_k[j], rw_k_a[j],
                             rw_v0[j - 1] if v_first is not None else jnp.zeros((d,), F32)])
            weights = [bf(rw_w_rkv[j, 0]), bf(rw_w_rkv[j, 1]), bf(rw_w_rkv[j, 2]),
                       bf(rw_w1[j]), bf(rw_w2[j]), bf(rw_a1[j]), bf(rw_a2[j]),
                       bf(rw_g1[j]), bf(rw_g2[j])]
            vres_w = None if v_first is None else [bf(rw_v1[j - 1]), bf(rw_v2[j - 1])]
            r, k, v, lw, kk, kka, g = _rwkv_in(xf, a1, sh1, rw_mu[j], vec, weights,
                                               v_first, vres_w, seq)
            if v_first is None:
                v_first = v
            xf = _wkv_mixer(r, k, v, lw, kk, kka, rw_r_k[j].reshape(1, d), rw_ln_w[j].reshape(1, d),
                            rw_ln_b[j].reshape(1, d), g, xf, gt1, bf(rw_w_o[j]), bsz, seq)
        xf = _mlp(xf, a2, sh2, gt2, fg, mlp_w1, mlp_w2, i, seq, i == depth - 1)
    return xf.reshape(bsz, seq, d)
```

```python
import functools

import jax
import jax.numpy as jnp
from jax import lax
from jax.experimental import pallas as pl
from jax.experimental.pallas import tpu as pltpu

F32 = jnp.float32
BF16 = jnp.bfloat16

HEAD_SIZE = 64
WKV_CHUNK = 64
GROUP_LANES = 256
VREG_LANES = 128
WKV_PAIR = 2
NORM_EPS = 1e-6
GN_EPS = 64e-5
VMEM_LIMIT_BYTES = 56 * 1024 * 1024

MLP_TM = 1024
MLP_TF = 1024
MLP_ROW_BLOCKS = 4
CONV_TM = 512
RWKV_IN_TM = 512
RWKV_IN_ROW_BLOCKS = 2
WKV_TT = 512
WKV_OUT_ROWS = 256


def _params(*semantics):
    return pltpu.CompilerParams(dimension_semantics=semantics,
                                vmem_limit_bytes=VMEM_LIMIT_BYTES)


def _dot(a, b):
    return jnp.dot(a, b, preferred_element_type=F32)


def _sigmoid(x):
    return 1.0 / (1.0 + jnp.exp2(x * (-1.4426950408889634)))


def _norm_mod(x, a, s):
    ms = jnp.mean(x * x, axis=-1, keepdims=True)
    return x * lax.rsqrt(ms + NORM_EPS) * a + s


def _head_mask():
    shape = (GROUP_LANES, GROUP_LANES)
    return (lax.broadcasted_iota(jnp.int32, shape, 0) // HEAD_SIZE
            == lax.broadcasted_iota(jnp.int32, shape, 1) // HEAD_SIZE)


def _head_ones():
    return jnp.where(_head_mask(), 1.0, 0.0).astype(BF16)


def _shift_rows(x, prev, n):
    out = pltpu.roll(x, n, 0)
    row = lax.broadcasted_iota(jnp.int32, (8, 1), 0)
    head = out[:8]
    for j in range(n):
        head = jnp.where(row == j, prev[8 - n + j:8 - n + j + 1], head)
    return jnp.concatenate([head, out[8:]], axis=0)


def _ada_kernel(c_ref, w_ref, b_ref, o_ref):
    c = c_ref[...]
    ca = (c * _sigmoid(c)).astype(BF16)
    o_ref[0] = _dot(ca, w_ref[0].astype(BF16)) + b_ref[0]


def _ada_mod(c, ada_w, ada_b):
    depth, d, n = ada_w.shape
    bsz = c.shape[0]
    rows = 16
    cp = jnp.zeros((rows, d), F32).at[:bsz].set(c)
    tn = min(n, 2048)
    out = pl.pallas_call(
        _ada_kernel,
        out_shape=jax.ShapeDtypeStruct((depth, rows, n), F32),
        grid=(depth, n // tn),
        in_specs=[pl.BlockSpec((rows, d), lambda i, j: (0, 0)),
                  pl.BlockSpec((1, d, tn), lambda i, j: (i, 0, j)),
                  pl.BlockSpec((1, 1, tn), lambda i, j: (i, 0, j))],
        out_specs=pl.BlockSpec((1, rows, tn), lambda i, j: (i, 0, j)),
        compiler_params=_params("parallel", "parallel"),
        name="ada_mod",
    )(cp, ada_w, ada_b.reshape(depth, 1, n))
    return out[:, :bsz]


def _mlp_kernel(x_ref, a_ref, s_ref, g_ref, fg_ref, w1_ref, w2_ref, o_ref, acc_ref, *, final_norm):
    f = pl.program_id(1)
    n_f = pl.num_programs(1)

    def step(first, last):
        w1 = w1_ref[0].astype(BF16)
        w2 = w2_ref[0].astype(BF16)
        rows_per_block = x_ref.shape[0] // MLP_ROW_BLOCKS
        for q in range(MLP_ROW_BLOCKS):
            rows = slice(q * rows_per_block, (q + 1) * rows_per_block)
            x = x_ref[rows]
            h = _norm_mod(x, a_ref[0], s_ref[0]).astype(BF16)
            z = jnp.maximum(_dot(h, w1), 0.0)
            part = _dot((z * z).astype(BF16), w2)
            acc = part if first else acc_ref[rows] + part
            if not last:
                acc_ref[rows] = acc
                continue
            y = x + g_ref[0] * acc
            if final_norm:
                ms = jnp.mean(y * y, axis=-1, keepdims=True)
                y = y * lax.rsqrt(ms + NORM_EPS) * fg_ref[...]
            o_ref[rows] = y

    pl.when(f == 0)(functools.partial(step, True, False))
    pl.when(jnp.logical_and(f > 0, f < n_f - 1))(functools.partial(step, False, False))
    pl.when(f == n_f - 1)(functools.partial(step, False, True))


def _mlp(x, a, s, g, fg, w1, w2, layer, seq, final_norm):
    m, d = x.shape
    ff = w1.shape[2]
    tm = min(seq, MLP_TM)
    tf = min(ff, MLP_TF)
    assert ff // tf >= 2
    per_seq = seq // tm
    vec = pl.BlockSpec((1, 1, d), lambda i, f: (i // per_seq, 0, 0))
    return pl.pallas_call(
        functools.partial(_mlp_kernel, final_norm=final_norm),
        out_shape=jax.ShapeDtypeStruct((m, d), F32),
        grid=(m // tm, ff // tf),
        in_specs=[pl.BlockSpec((tm, d), lambda i, f: (i, 0)), vec, vec, vec,
                  pl.BlockSpec((1, d), lambda i, f: (0, 0)),
                  pl.BlockSpec((1, d, tf), lambda i, f: (layer, 0, f)),
                  pl.BlockSpec((1, tf, d), lambda i, f: (layer, f, 0))],
        out_specs=pl.BlockSpec((tm, d), lambda i, f: (i, 0)),
        scratch_shapes=[pltpu.VMEM((tm, d), F32)],
        compiler_params=_params("parallel", "arbitrary"),
        name="mlp",
    )(x, a, s, g, fg, w1, w2)


def _conv_kernel(x_ref, a_ref, s_ref, g_ref, win_ref, cw_ref, wo_ref, o_ref, carry_ref, *, per_seq):
    i = pl.program_id(0)
    tm, d = x_ref.shape

    @pl.when(i % per_seq == 0)
    def _():
        carry_ref[...] = jnp.zeros_like(carry_ref)

    x = x_ref[...]
    h = _norm_mod(x, a_ref[0], s_ref[0]).astype(BF16)
    u = _dot(h, win_ref[:, d:2 * d]) * _dot(h, win_ref[:, 2 * d:])
    prev = carry_ref[...]
    cw = cw_ref[...]
    conv = cw[0:1] * _shift_rows(u, prev, 2) + cw[1:2] * _shift_rows(u, prev, 1) + cw[2:3] * u
    carry_ref[...] = u[tm - 8:tm]
    y = (_dot(h, win_ref[:, :d]) * conv).astype(BF16)
    o_ref[...] = x + g_ref[0] * _dot(y, wo_ref[...])


def _conv_mixer(x, a, s, g, w_in, cw, wo, seq):
    m, d = x.shape
    tm = min(seq, CONV_TM)
    per_seq = seq // tm
    tile = pl.BlockSpec((tm, d), lambda i: (i, 0))
    vec = pl.BlockSpec((1, 1, d), lambda i: (i // per_seq, 0, 0))
    return pl.pallas_call(
        functools.partial(_conv_kernel, per_seq=per_seq),
        out_shape=jax.ShapeDtypeStruct((m, d), F32),
        grid=(m // tm,),
        in_specs=[tile, vec, vec, vec,
                  pl.BlockSpec(w_in.shape, lambda i: (0, 0)),
                  pl.BlockSpec(cw.shape, lambda i: (0, 0)),
                  pl.BlockSpec((d, d), lambda i: (0, 0))],
        out_specs=tile,
        scratch_shapes=[pltpu.VMEM((8, d), F32)],
        compiler_params=_params("arbitrary"),
        name="conv_mixer",
    )(x, a, s, g, w_in, cw, wo)


def _rwkv_in_kernel(*refs, per_seq, use_vres):
    if use_vres:
        (x_ref, a_ref, s_ref, mu_ref, vec_ref, wr_ref, wk_ref, wv_ref, w1_ref, w2_ref,
         a1_ref, a2_ref, g1_ref, g2_ref, vf_ref, v1_ref, v2_ref,
         r_ref, k_ref, v_ref, lw_ref, kk_ref, kka_ref, g_ref, carry_ref) = refs
    else:
        (x_ref, a_ref, s_ref, mu_ref, vec_ref, wr_ref, wk_ref, wv_ref, w1_ref, w2_ref,
         a1_ref, a2_ref, g1_ref, g2_ref,
         r_ref, k_ref, v_ref, lw_ref, kk_ref, kka_ref, g_ref, carry_ref) = refs
    i = pl.program_id(0)
    tm, d = x_ref.shape

    @pl.when(i % per_seq == 0)
    def _():
        carry_ref[...] = jnp.zeros_like(carry_ref)

    mu = mu_ref[...]
    vec = vec_ref[...]
    ones = _head_ones()
    prev = carry_ref[...]

    block = tm // RWKV_IN_ROW_BLOCKS
    for q in range(RWKV_IN_ROW_BLOCKS):
        rows = slice(q * block, (q + 1) * block)
        h = _norm_mod(x_ref[rows], a_ref[0], s_ref[0])
        xx = _shift_rows(h, prev, 1) - h
        prev = h[block - 8:block]

        def mix(n):
            return (h + xx * mu[n:n + 1]).astype(BF16)

        r_ref[rows] = _dot(mix(0), wr_ref[...])

        wl = vec[0:1] + _dot(jnp.tanh(_dot(mix(1), w1_ref[...])).astype(BF16), w2_ref[...])
        lw_ref[rows] = -jnp.exp(-0.5) * _sigmoid(wl)

        g_ref[rows] = _dot(_sigmoid(_dot(mix(5), g1_ref[...])).astype(BF16), g2_ref[...])

        xv = mix(3)
        v = _dot(xv, wv_ref[...])
        if use_vres:
            mix_v = _sigmoid(vec[4:5] + _dot(_dot(xv, v1_ref[...]).astype(BF16), v2_ref[...]))
            v = v + (vf_ref[rows] - v) * mix_v
        v_ref[rows] = v

        a = _sigmoid(vec[1:2] + _dot(_dot(mix(4), a1_ref[...]).astype(BF16), a2_ref[...]))
        k = _dot(mix(2), wk_ref[...])
        k_ref[rows] = k * ((1.0 - vec[3:4]) + a * vec[3:4])
        kk = k * vec[2:3]
        sq = (kk * kk).astype(BF16)
        ss = jnp.concatenate([_dot(sq[:, n * GROUP_LANES:(n + 1) * GROUP_LANES], ones)
                              for n in range(d // GROUP_LANES)], axis=1)
        kk = kk * jnp.minimum(lax.rsqrt(ss), 1e12)
        kk_ref[rows] = kk
        kka_ref[rows] = kk * a
    carry_ref[...] = prev


def _rwkv_in(x, a, s, mu, vec, weights, vfirst, vres_w, seq):
    m, d = x.shape
    tm = min(seq, RWKV_IN_TM)
    per_seq = seq // tm
    use_vres = vfirst is not None
    tile = pl.BlockSpec((tm, d), lambda i: (i, 0))
    mvec = pl.BlockSpec((1, 1, d), lambda i: (i // per_seq, 0, 0))

    def full(arr):
        return pl.BlockSpec(arr.shape, lambda i: (0,) * arr.ndim, pipeline_mode=pl.Buffered(1))

    args = [x, a, s, mu, vec] + list(weights)
    specs = [tile, mvec, mvec, full(mu), full(vec)] + [full(w) for w in weights]
    if use_vres:
        args += [vfirst] + list(vres_w)
        specs += [tile] + [full(w) for w in vres_w]
    out = jax.ShapeDtypeStruct((m, d), F32)
    return pl.pallas_call(
        functools.partial(_rwkv_in_kernel, per_seq=per_seq, use_vres=use_vres),
        out_shape=(out,) * 7,
        grid=(m // tm,),
        in_specs=specs,
        out_specs=(tile,) * 7,
        scratch_shapes=[pltpu.VMEM((8, d), F32)],
        compiler_params=_params("arbitrary"),
        name="rwkv_in",
    )(*args)


def _block_diag(x):
    n_rows, width = x.shape
    xb = x.astype(BF16)
    lane_head = lax.broadcasted_iota(jnp.int32, (n_rows, VREG_LANES), 1) // HEAD_SIZE
    zero = jnp.zeros((n_rows, VREG_LANES), BF16)
    n_tiles = width // VREG_LANES
    blocks = []
    for h in range(width // HEAD_SIZE):
        tile, sub = divmod(h * HEAD_SIZE, VREG_LANES)
        piece = jnp.where(lane_head == sub // HEAD_SIZE,
                          xb[:, tile * VREG_LANES:(tile + 1) * VREG_LANES], zero)
        blocks.append(jnp.concatenate([piece if n == tile else zero for n in range(n_tiles)], axis=1))
    return jnp.concatenate(blocks, axis=0)


def _block_diag_t(x):
    n_rows, width = x.shape
    reps = VREG_LANES // n_rows
    shape = (VREG_LANES, VREG_LANES)
    keep = (lax.broadcasted_iota(jnp.int32, shape, 0) // HEAD_SIZE
            == lax.broadcasted_iota(jnp.int32, shape, 1) // n_rows)
    n_tiles = width // VREG_LANES
    zero = jnp.zeros(shape, BF16)
    blocks = []
    for tile in range(n_tiles):
        xt = x[:, tile * VREG_LANES:(tile + 1) * VREG_LANES]
        sq = jnp.where(keep, jnp.concatenate([xt] * reps, axis=0).T, 0.0).astype(BF16)
        blocks.append(jnp.concatenate([sq if n == tile else zero for n in range(n_tiles)], axis=1))
    return jnp.concatenate(blocks, axis=0)


def _wkv_kernel(r_ref, k_ref, v_ref, lw_ref, kk_ref, kka_ref, rk_ref, lnw_ref, lnb_ref,
                g_ref, x_ref, gate_ref, wo_ref,
                o_ref, state_ref, wr_s, u0_s, arb_s, ark_s, bkt_s, gm_s):
    tt, d = r_ref.shape
    c_len, gl = WKV_CHUNK, GROUP_LANES
    n_pairs = tt // (c_len * WKV_PAIR)
    n_groups = d // gl
    n_sq = (c_len - 1).bit_length() - 1

    @pl.when(pl.program_id(1) == 0)
    def _():
        state_ref[...] = jnp.zeros_like(state_ref)

    row = lax.broadcasted_iota(jnp.int32, (c_len, gl), 0)
    col = lax.broadcasted_iota(jnp.int32, (c_len, gl), 1) % c_len
    strict = col < row
    incl = col <= row
    eye = jnp.where(col == row, 1.0, 0.0)
    bd_mask = _head_mask()
    tri = jnp.where(lax.broadcasted_iota(jnp.int32, (c_len, c_len), 1)
                    <= lax.broadcasted_iota(jnp.int32, (c_len, c_len), 0), 1.0, 0.0).astype(BF16)
    tri3 = jnp.concatenate([tri, tri, tri], axis=1)
    head_ones = jnp.where(bd_mask, 1.0, 0.0).astype(BF16)
    inv_n = 1.0 / HEAD_SIZE

    def bf(x):
        return x.astype(BF16)

    def lanes(gi):
        return slice(gi * gl, (gi + 1) * gl)

    def slot_of(c, gi):
        ring = 2 * WKV_PAIR
        return (c % ring if isinstance(c, int) else lax.rem(c, ring)) * n_groups + gi

    def chunk_rows(c):
        start = c * c_len
        return pl.ds(start if isinstance(start, int) else pl.multiple_of(start, c_len), c_len)

    def prepare(p):
        items = [(p * WKV_PAIR + ci, gi) for ci in range(WKV_PAIR) for gi in range(n_groups)]

        cums = []
        for c, gi in items:
            lw = lw_ref[chunk_rows(c), lanes(gi)]
            p0 = lw.astype(BF16)
            rem = lw - p0.astype(F32)
            p1 = rem.astype(BF16)
            p2 = (rem - p1.astype(F32)).astype(BF16)
            cums.append(_dot(tri3, jnp.concatenate([p0, p1, p2], axis=0)))
        yield

        at, lhs, bt, kt = [], [], [], []
        for (c, gi), cum in zip(items, cums):
            rs, ls, slot = chunk_rows(c), lanes(gi), slot_of(c, gi)
            c_last = cum[c_len - 1:c_len]
            kka = kka_ref[rs, ls]
            kx = k_ref[rs, ls]
            e_inv = jnp.exp(-cum)
            e_tail = jnp.exp(c_last - cum)
            a_t = -kk_ref[rs, ls] * jnp.exp(cum - lw_ref[rs, ls])
            r_t = r_ref[rs, ls] * jnp.exp(cum)
            at.append(bf(a_t))
            lhs.append(bf(jnp.concatenate([a_t, r_t], axis=0)))
            bt.append(kka * e_inv)
            kt.append(kx * e_inv)
            bkt_s[slot] = bf(jnp.concatenate([kka * e_tail, kx * e_tail], axis=0).T)
            gm_s[slot] = jnp.broadcast_to(jnp.exp(c_last), (VREG_LANES, gl)).T

        sb = [_dot(l, _block_diag_t(b)) for l, b in zip(lhs, bt)]
        sk = [_dot(l, _block_diag_t(k)) for l, k in zip(lhs, kt)]
        for (c, gi), b, k in zip(items, sb, sk):
            arb_s[slot_of(c, gi)] = bf(jnp.where(incl, b[c_len:], 0.0))
            ark_s[slot_of(c, gi)] = bf(jnp.where(incl, k[c_len:], 0.0))
        yield

        xs = [jnp.where(strict, b[:c_len], 0.0) for b in sb]
        ts = [eye + x for x in xs]
        xs = [_dot(bf(x), _block_diag(x)) for x in xs]
        yield
        for _ in range(n_sq - 1):
            res = [_dot(bf(jnp.concatenate([x, t], axis=0)), _block_diag(x)) for x, t in zip(xs, ts)]
            xs = [r[:c_len] for r in res]
            ts = [t + r[c_len:] for t, r in zip(ts, res)]
            yield
        ts = [bf(t + _dot(bf(t), _block_diag(x))) for x, t in zip(xs, ts)]
        yield

        akv = [_dot(bf(jnp.where(strict, k[:c_len], 0.0)), _block_diag(v_ref[chunk_rows(c), lanes(gi)]))
               for (c, gi), k in zip(items, sk)]
        ws = [_dot(t, _block_diag(a)) for t, a in zip(ts, at)]
        u0 = [_dot(t, _block_diag(a)) for t, a in zip(ts, akv)]
        for (c, gi), w, u, l in zip(items, ws, u0, lhs):
            wr_s[slot_of(c, gi)] = jnp.concatenate([bf(w), l[c_len:]], axis=0)
            u0_s[slot_of(c, gi)] = u

    def advance_chunk(c):
        rs = chunk_rows(c)
        slots = [slot_of(c, gi) for gi in range(n_groups)]
        st = [state_ref[gi] for gi in range(n_groups)]
        vs = [v_ref[rs, lanes(gi)] for gi in range(n_groups)]
        ps = [_dot(wr_s[s], bf(x)) for s, x in zip(slots, st)]
        yield
        us = [u0_s[s] + p[:c_len] for s, p in zip(slots, ps)]
        upd = [_dot(bkt_s[s], bf(jnp.concatenate([u, v], axis=0))) for s, u, v in zip(slots, us, vs)]
        for gi in range(n_groups):
            gm = gm_s[slots[gi]]
            state_ref[gi] = (st[gi] * jnp.concatenate([gm] * (gl // VREG_LANES), axis=1)
                             + jnp.where(bd_mask, upd[gi], 0.0))
        yield
        ys = [p[c_len:] + _dot(arb_s[s], _block_diag(u)) + _dot(ark_s[s], _block_diag(v))
              for s, p, u, v in zip(slots, ps, us, vs)]
        rows_in = []
        for gi, y in enumerate(ys):
            rows_in += [y, r_ref[rs, lanes(gi)] * k_ref[rs, lanes(gi)] * rk_ref[:, lanes(gi)]]
        sums = _dot(bf(jnp.concatenate(rows_in, axis=0)), head_ones)
        yield
        devs = [y - sums[2 * gi * c_len:(2 * gi + 1) * c_len] * inv_n for gi, y in enumerate(ys)]
        var = _dot(bf(jnp.concatenate([dv * dv for dv in devs], axis=0)), head_ones) * inv_n
        for gi in range(n_groups):
            ls = lanes(gi)
            o_ref[rs, ls] = (devs[gi] * lax.rsqrt(var[gi * c_len:(gi + 1) * c_len] + GN_EPS)
                             * lnw_ref[:, ls] + lnb_ref[:, ls]
                             + sums[(2 * gi + 1) * c_len:(2 * gi + 2) * c_len] * vs[gi])

    def run(stages, order):
        for name in order:
            next(stages[name], None)

    assert WKV_PAIR == 2
    run({"p": prepare(0)}, "p" * (n_sq + 4))

    def body(p, carry):
        stages = {"p": prepare(p), "a": advance_chunk((p - 1) * WKV_PAIR),
                  "b": advance_chunk((p - 1) * WKV_PAIR + 1)}
        run(stages, "apapbppbpapapbpbp" + "p" * (n_sq - 5))
        return carry

    for p in range(1, n_pairs):
        body(p, 0)
    last = (n_pairs - 1) * WKV_PAIR
    run({"a": advance_chunk(last), "b": advance_chunk(last + 1)}, "aabababb")

    block = min(tt, WKV_OUT_ROWS)
    for q in range(tt // block):
        rows = slice(q * block, (q + 1) * block)
        z = (o_ref[rows] * g_ref[rows]).astype(BF16)
        o_ref[rows] = x_ref[rows] + gate_ref[0] * _dot(z, wo_ref[...])


def _wkv_mixer(r, k, v, lw, kk, kka, rk, lnw, lnb, g, x, gate, wo, bsz, seq):
    m, d = r.shape
    tt = min(seq, WKV_TT)
    assert tt % (WKV_CHUNK * WKV_PAIR) == 0
    per_seq = seq // tt
    n_groups = d // GROUP_LANES
    n_slots = 2 * WKV_PAIR * n_groups
    tile = pl.BlockSpec((tt, d), lambda b, t: (b * per_seq + t, 0))
    vec = pl.BlockSpec((1, d), lambda b, t: (0, 0))
    return pl.pallas_call(
        _wkv_kernel,
        out_shape=jax.ShapeDtypeStruct((m, d), F32),
        grid=(bsz, per_seq),
        in_specs=[tile] * 6 + [vec] * 3 + [tile, tile,
                  pl.BlockSpec((1, 1, d), lambda b, t: (b, 0, 0)),
                  pl.BlockSpec((d, d), lambda b, t: (0, 0), pipeline_mode=pl.Buffered(1))],
        out_specs=tile,
        scratch_shapes=[pltpu.VMEM((n_groups, GROUP_LANES, GROUP_LANES), F32),
                        pltpu.VMEM((n_slots, 2 * WKV_CHUNK, GROUP_LANES), BF16),
                        pltpu.VMEM((n_slots, WKV_CHUNK, GROUP_LANES), F32),
                        pltpu.VMEM((n_slots, WKV_CHUNK, GROUP_LANES), BF16),
                        pltpu.VMEM((n_slots, WKV_CHUNK, GROUP_LANES), BF16),
                        pltpu.VMEM((n_slots, GROUP_LANES, 2 * WKV_CHUNK), BF16),
                        pltpu.VMEM((n_slots, GROUP_LANES, VREG_LANES), F32)],
        compiler_params=_params("parallel", "arbitrary"),
        name="wkv_mixer",
    )(r, k, v, lw, kk, kka, rk, lnw, lnb, g, x, gate, wo)


def kernel(x, c, norm_g, final_g, ada_w, ada_b, conv_w_in, conv_w, conv_w_out, rw_mu, rw_w_rkv, rw_w_o, rw_w0, rw_w1, rw_w2, rw_a0, rw_a1, rw_a2, rw_g1, rw_g2, rw_k_k, rw_k_a, rw_r_k, rw_ln_w, rw_ln_b, rw_v0, rw_v1, rw_v2, mlp_w1, mlp_w2):
    bsz, seq, d = x.shape
    depth = ada_w.shape[0]
    assert d % GROUP_LANES == 0 and seq % WKV_CHUNK == 0
    bf = lambda w: w.astype(BF16)

    mod = _ada_mod(c, ada_w, ada_b).reshape(depth, bsz, 6, 1, d)
    fg = final_g.reshape(1, d)
    xf = x.reshape(bsz * seq, d)
    v_first = None
    for i in range(depth):
        sh1, sc1, gt1, sh2, sc2, gt2 = (mod[i, :, n] for n in range(6))
        a1 = norm_g[i, 0] * (1.0 + sc1)
        a2 = norm_g[i, 1] * (1.0 + sc2)
        j = i // 2
        if i % 2 == 0:
            xf = _conv_mixer(xf, a1, sh1, gt1, bf(conv_w_in[j]), conv_w[j], bf(conv_w_out[j]), seq)
        else:
            vec = jnp.stack([rw_w0[j], rw_a0[j], rw_k_k[j], rw_k_a[j],
                             rw_v0[j - 1] if v_first is not None else jnp.zeros((d,), F32)])
            weights = [bf(rw_w_rkv[j, 0]), bf(rw_w_rkv[j, 1]), bf(rw_w_rkv[j, 2]),
                       bf(rw_w1[j]), bf(rw_w2[j]), bf(rw_a1[j]), bf(rw_a2[j]),
                       bf(rw_g1[j]), bf(rw_g2[j])]
            vres_w = None if v_first is None else [bf(rw_v1[j - 1]), bf(rw_v2[j - 1])]
            r, k, v, lw, kk, kka, g = _rwkv_in(xf, a1, sh1, rw_mu[j], vec, weights,
                                               v_first, vres_w, seq)
            if v_first is None:
                v_first = v
            xf = _wkv_mixer(r, k, v, lw, kk, kka, rw_r_k[j].reshape(1, d), rw_ln_w[j].reshape(1, d),
                            rw_ln_b[j].reshape(1, d), g, xf, gt1, bf(rw_w_o[j]), bsz, seq)
        xf = _mlp(xf, a2, sh2, gt2, fg, mlp_w1, mlp_w2, i, seq, i == depth - 1)
    return xf.reshape(bsz, seq, d)
```

```python
import functools

import jax
import jax.numpy as jnp
from jax import lax
from jax.experimental import pallas as pl
from jax.experimental.pallas import tpu as pltpu

F32 = jnp.float32
BF16 = jnp.bfloat16

HEAD_SIZE = 64
WKV_CHUNK = 64
GROUP_LANES = 256
VREG_LANES = 128
WKV_PAIR = 2
NORM_EPS = 1e-6
GN_EPS = 64e-5
VMEM_LIMIT_BYTES = 56 * 1024 * 1024

MLP_TM = 1024
MLP_TF = 1024
MLP_ROW_BLOCKS = 4
CONV_TM = 512
RWKV_IN_TM = 512
RWKV_IN_ROW_BLOCKS = 2
WKV_TT = 512
WKV_OUT_ROWS = 256


def _params(*semantics):
    return pltpu.CompilerParams(dimension_semantics=semantics,
                                vmem_limit_bytes=VMEM_LIMIT_BYTES)


def _dot(a, b):
    return jnp.dot(a, b, preferred_element_type=F32)


def _sigmoid(x):
    return 1.0 / (1.0 + jnp.exp2(x * (-1.4426950408889634)))


def _norm_mod(x, a, s):
    ms = jnp.mean(x * x, axis=-1, keepdims=True)
    return x * lax.rsqrt(ms + NORM_EPS) * a + s


def _head_mask():
    shape = (GROUP_LANES, GROUP_LANES)
    return (lax.broadcasted_iota(jnp.int32, shape, 0) // HEAD_SIZE
            == lax.broadcasted_iota(jnp.int32, shape, 1) // HEAD_SIZE)


def _head_ones():
    return jnp.where(_head_mask(), 1.0, 0.0).astype(BF16)


def _shift_rows(x, prev, n):
    out = pltpu.roll(x, n, 0)
    row = lax.broadcasted_iota(jnp.int32, (8, 1), 0)
    head = out[:8]
    for j in range(n):
        head = jnp.where(row == j, prev[8 - n + j:8 - n + j + 1], head)
    return jnp.concatenate([head, out[8:]], axis=0)


def _ada_kernel(c_ref, w_ref, b_ref, o_ref):
    c = c_ref[...]
    ca = (c * _sigmoid(c)).astype(BF16)
    o_ref[0] = _dot(ca, w_ref[0].astype(BF16)) + b_ref[0]


def _ada_mod(c, ada_w, ada_b):
    depth, d, n = ada_w.shape
    bsz = c.shape[0]
    rows = 16
    cp = jnp.zeros((rows, d), F32).at[:bsz].set(c)
    tn = min(n, 2048)
    out = pl.pallas_call(
        _ada_kernel,
        out_shape=jax.ShapeDtypeStruct((depth, rows, n), F32),
        grid=(depth, n // tn),
        in_specs=[pl.BlockSpec((rows, d), lambda i, j: (0, 0)),
                  pl.BlockSpec((1, d, tn), lambda i, j: (i, 0, j)),
                  pl.BlockSpec((1, 1, tn), lambda i, j: (i, 0, j))],
        out_specs=pl.BlockSpec((1, rows, tn), lambda i, j: (i, 0, j)),
        compiler_params=_params("parallel", "parallel"),
        name="ada_mod",
    )(cp, ada_w, ada_b.reshape(depth, 1, n))
    return out[:, :bsz]


def _mlp_kernel(x_ref, a_ref, s_ref, g_ref, fg_ref, w1_ref, w2_ref, o_ref, acc_ref, *, final_norm):
    f = pl.program_id(1)
    n_f = pl.num_programs(1)

    def step(first, last):
        w1 = w1_ref[0].astype(BF16)
        w2 = w2_ref[0].astype(BF16)
        rows_per_block = x_ref.shape[0] // MLP_ROW_BLOCKS
        for q in range(MLP_ROW_BLOCKS):
            rows = slice(q * rows_per_block, (q + 1) * rows_per_block)
            x = x_ref[rows]
            h = _norm_mod(x, a_ref[0], s_ref[0]).astype(BF16)
            z = jnp.maximum(_dot(h, w1), 0.0)
            part = _dot((z * z).astype(BF16), w2)
            acc = part if first else acc_ref[rows] + part
            if not last:
                acc_ref[rows] = acc
                continue
            y = x + g_ref[0] * acc
            if final_norm:
                ms = jnp.mean(y * y, axis=-1, keepdims=True)
                y = y * lax.rsqrt(ms + NORM_EPS) * fg_ref[...]
            o_ref[rows] = y

    pl.when(f == 0)(functools.partial(step, True, False))
    pl.when(jnp.logical_and(f > 0, f < n_f - 1))(functools.partial(step, False, False))
    pl.when(f == n_f - 1)(functools.partial(step, False, True))


def _mlp(x, a, s, g, fg, w1, w2, layer, seq, final_norm):
    m, d = x.shape
    ff = w1.shape[2]
    tm = min(seq, MLP_TM)
    tf = min(ff, MLP_TF)
    assert ff // tf >= 2
    per_seq = seq // tm
    vec = pl.BlockSpec((1, 1, d), lambda i, f: (i // per_seq, 0, 0))
    return pl.pallas_call(
        functools.partial(_mlp_kernel, final_norm=final_norm),
        out_shape=jax.ShapeDtypeStruct((m, d), F32),
        grid=(m // tm, ff // tf),
        in_specs=[pl.BlockSpec((tm, d), lambda i, f: (i, 0)), vec, vec, vec,
                  pl.BlockSpec((1, d), lambda i, f: (0, 0)),
                  pl.BlockSpec((1, d, tf), lambda i, f: (layer, 0, f)),
                  pl.BlockSpec((1, tf, d), lambda i, f: (layer, f, 0))],
        out_specs=pl.BlockSpec((tm, d), lambda i, f: (i, 0)),
        scratch_shapes=[pltpu.VMEM((tm, d), F32)],
        compiler_params=_params("parallel", "arbitrary"),
        name="mlp",
    )(x, a, s, g, fg, w1, w2)


def _conv_kernel(x_ref, a_ref, s_ref, g_ref, win_ref, cw_ref, wo_ref, o_ref, carry_ref, *, per_seq):
    i = pl.program_id(0)
    tm, d = x_ref.shape

    @pl.when(i % per_seq == 0)
    def _():
        carry_ref[...] = jnp.zeros_like(carry_ref)

    x = x_ref[...]
    h = _norm_mod(x, a_ref[0], s_ref[0]).astype(BF16)
    u = _dot(h, win_ref[:, d:2 * d]) * _dot(h, win_ref[:, 2 * d:])
    prev = carry_ref[...]
    cw = cw_ref[...]
    conv = cw[0:1] * _shift_rows(u, prev, 2) + cw[1:2] * _shift_rows(u, prev, 1) + cw[2:3] * u
    carry_ref[...] = u[tm - 8:tm]
    y = (_dot(h, win_ref[:, :d]) * conv).astype(BF16)
    o_ref[...] = x + g_ref[0] * _dot(y, wo_ref[...])


def _conv_mixer(x, a, s, g, w_in, cw, wo, seq):
    m, d = x.shape
    tm = min(seq, CONV_TM)
    per_seq = seq // tm
    tile = pl.BlockSpec((tm, d), lambda i: (i, 0))
    vec = pl.BlockSpec((1, 1, d), lambda i: (i // per_seq, 0, 0))
    return pl.pallas_call(
        functools.partial(_conv_kernel, per_seq=per_seq),
        out_shape=jax.ShapeDtypeStruct((m, d), F32),
        grid=(m // tm,),
        in_specs=[tile, vec, vec, vec,
                  pl.BlockSpec(w_in.shape, lambda i: (0, 0)),
                  pl.BlockSpec(cw.shape, lambda i: (0, 0)),
                  pl.BlockSpec((d, d), lambda i: (0, 0))],
        out_specs=tile,
        scratch_shapes=[pltpu.VMEM((8, d), F32)],
        compiler_params=_params("arbitrary"),
        name="conv_mixer",
    )(x, a, s, g, w_in, cw, wo)


def _rwkv_in_kernel(*refs, per_seq, use_vres):
    if use_vres:
        (x_ref, a_ref, s_ref, mu_ref, vec_ref, wr_ref, wk_ref, wv_ref, w1_ref, w2_ref,
         a1_ref, a2_ref, g1_ref, g2_ref, vf_ref, v1_ref, v2_ref,
         r_ref, k_ref, v_ref, lw_ref, kk_ref, kka_ref, g_ref, carry_ref) = refs
    else:
        (x_ref, a_ref, s_ref, mu_ref, vec_ref, wr_ref, wk_ref, wv_ref, w1_ref, w2_ref,
         a1_ref, a2_ref, g1_ref, g2_ref,
         r_ref, k_ref, v_ref, lw_ref, kk_ref, kka_ref, g_ref, carry_ref) = refs
    i = pl.program_id(0)
    tm, d = x_ref.shape

    @pl.when(i % per_seq == 0)
    def _():
        carry_ref[...] = jnp.zeros_like(carry_ref)

    mu = mu_ref[...]
    vec = vec_ref[...]
    ones = _head_ones()
    prev = carry_ref[...]

    block = tm // RWKV_IN_ROW_BLOCKS
    for q in range(RWKV_IN_ROW_BLOCKS):
        rows = slice(q * block, (q + 1) * block)
        h = _norm_mod(x_ref[rows], a_ref[0], s_ref[0])
        xx = _shift_rows(h, prev, 1) - h
        prev = h[block - 8:block]

        def mix(n):
            return (h + xx * mu[n:n + 1]).astype(BF16)

        r_ref[rows] = _dot(mix(0), wr_ref[...])

        wl = vec[0:1] + _dot(jnp.tanh(_dot(mix(1), w1_ref[...])).astype(BF16), w2_ref[...])
        lw_ref[rows] = -jnp.exp(-0.5) * _sigmoid(wl)

        g_ref[rows] = _dot(_sigmoid(_dot(mix(5), g1_ref[...])).astype(BF16), g2_ref[...])

        xv = mix(3)
        v = _dot(xv, wv_ref[...])
        if use_vres:
            mix_v = _sigmoid(vec[4:5] + _dot(_dot(xv, v1_ref[...]).astype(BF16), v2_ref[...]))
            v = v + (vf_ref[rows] - v) * mix_v
        v_ref[rows] = v

        a = _sigmoid(vec[1:2] + _dot(_dot(mix(4), a1_ref[...]).astype(BF16), a2_ref[...]))
        k = _dot(mix(2), wk_ref[...])
        k_ref[rows] = k * ((1.0 - vec[3:4]) + a * vec[3:4])
        kk = k * vec[2:3]
        sq = (kk * kk).astype(BF16)
        ss = jnp.concatenate([_dot(sq[:, n * GROUP_LANES:(n + 1) * GROUP_LANES], ones)
                              for n in range(d // GROUP_LANES)], axis=1)
        kk = kk * jnp.minimum(lax.rsqrt(ss), 1e12)
        kk_ref[rows] = kk
        kka_ref[rows] = kk * a
    carry_ref[...] = prev


def _rwkv_in(x, a, s, mu, vec, weights, vfirst, vres_w, seq):
    m, d = x.shape
    tm = min(seq, RWKV_IN_TM)
    per_seq = seq // tm
    use_vres = vfirst is not None
    tile = pl.BlockSpec((tm, d), lambda i: (i, 0))
    mvec = pl.BlockSpec((1, 1, d), lambda i: (i // per_seq, 0, 0))

    def full(arr):
        return pl.BlockSpec(arr.shape, lambda i: (0,) * arr.ndim, pipeline_mode=pl.Buffered(1))

    args = [x, a, s, mu, vec] + list(weights)
    specs = [tile, mvec, mvec, full(mu), full(vec)] + [full(w) for w in weights]
    if use_vres:
        args += [vfirst] + list(vres_w)
        specs += [tile] + [full(w) for w in vres_w]
    out = jax.ShapeDtypeStruct((m, d), F32)
    return pl.pallas_call(
        functools.partial(_rwkv_in_kernel, per_seq=per_seq, use_vres=use_vres),
        out_shape=(out,) * 7,
        grid=(m // tm,),
        in_specs=specs,
        out_specs=(tile,) * 7,
        scratch_shapes=[pltpu.VMEM((8, d), F32)],
        compiler_params=_params("arbitrary"),
        name="rwkv_in",
    )(*args)


def _block_diag(x):
    n_rows, width = x.shape
    xb = x.astype(BF16)
    lane_head = lax.broadcasted_iota(jnp.int32, (n_rows, VREG_LANES), 1) // HEAD_SIZE
    zero = jnp.zeros((n_rows, VREG_LANES), BF16)
    n_tiles = width // VREG_LANES
    blocks = []
    for h in range(width // HEAD_SIZE):
        tile, sub = divmod(h * HEAD_SIZE, VREG_LANES)
        piece = jnp.where(lane_head == sub // HEAD_SIZE,
                          xb[:, tile * VREG_LANES:(tile + 1) * VREG_LANES], zero)
        blocks.append(jnp.concatenate([piece if n == tile else zero for n in range(n_tiles)], axis=1))
    return jnp.concatenate(blocks, axis=0)


def _block_diag_t(x):
    n_rows, width = x.shape
    reps = VREG_LANES // n_rows
    shape = (VREG_LANES, VREG_LANES)
    keep = (lax.broadcasted_iota(jnp.int32, shape, 0) // HEAD_SIZE
            == lax.broadcasted_iota(jnp.int32, shape, 1) // n_rows)
    n_tiles = width // VREG_LANES
    zero = jnp.zeros(shape, BF16)
    blocks = []
    for tile in range(n_tiles):
        xt = x[:, tile * VREG_LANES:(tile + 1) * VREG_LANES]
        sq = jnp.where(keep, jnp.concatenate([xt] * reps, axis=0).T.astype(BF16), zero)
        blocks.append(jnp.concatenate([sq if n == tile else zero for n in range(n_tiles)], axis=1))
    return jnp.concatenate(blocks, axis=0)


def _wkv_kernel(r_ref, k_ref, v_ref, lw_ref, kk_ref, kka_ref, rk_ref, lnw_ref, lnb_ref,
                g_ref, x_ref, gate_ref, wo_ref,
                o_ref, state_ref, wr_s, u0_s, arb_s, ykv_s, bkt_s, gm_s):
    tt, d = r_ref.shape
    c_len, gl = WKV_CHUNK, GROUP_LANES
    n_pairs = tt // (c_len * WKV_PAIR)
    n_groups = d // gl
    n_sq = (c_len - 1).bit_length() - 1

    @pl.when(pl.program_id(1) == 0)
    def _():
        state_ref[...] = jnp.zeros_like(state_ref)

    row = lax.broadcasted_iota(jnp.int32, (c_len, gl), 0)
    col = lax.broadcasted_iota(jnp.int32, (c_len, gl), 1) % c_len
    row8 = lax.broadcasted_iota(jnp.int32, (8, gl), 0)
    lane_head = lax.broadcasted_iota(jnp.int32, (HEAD_SIZE, VREG_LANES), 1) // HEAD_SIZE
    strict = col < row
    incl = col <= row
    eye = jnp.where(col == row, 1.0, 0.0)
    bd_mask = _head_mask()
    head_ones = jnp.where(bd_mask, 1.0, 0.0).astype(BF16)
    inv_n = 1.0 / HEAD_SIZE

    def bf(x):
        return x.astype(BF16)

    head_blocks = []
    for h in range(gl // HEAD_SIZE):
        tile, sub = divmod(h * HEAD_SIZE, VREG_LANES)
        head_blocks.append((slice(h * HEAD_SIZE, (h + 1) * HEAD_SIZE),
                            slice(tile * VREG_LANES, (tile + 1) * VREG_LANES), sub // HEAD_SIZE))

    def state_operand(pieces):
        zero = jnp.zeros((HEAD_SIZE, VREG_LANES), BF16)
        n_tiles = gl // VREG_LANES
        return jnp.concatenate(
            [jnp.concatenate([bf(p) if n * VREG_LANES == tl.start else zero for n in range(n_tiles)], axis=1)
             for p, (_, tl, _) in zip(pieces, head_blocks)], axis=0)

    def lanes(gi):
        return slice(gi * gl, (gi + 1) * gl)

    def slot_of(c, gi):
        ring = 2 * WKV_PAIR
        return (c % ring if isinstance(c, int) else lax.rem(c, ring)) * n_groups + gi

    def chunk_rows(c):
        start = c * c_len
        return pl.ds(start if isinstance(start, int) else pl.multiple_of(start, c_len), c_len)

    def prepare(p):
        items = [(p * WKV_PAIR + ci, gi) for ci in range(WKV_PAIR) for gi in range(n_groups)]

        cums = []
        for c, gi in items:
            lw = lw_ref[chunk_rows(c), lanes(gi)]
            parts, total = [], None
            for j in range(c_len // 8):
                grp = lw[8 * j:8 * j + 8]
                for shift in (1, 2, 4):
                    grp = grp + jnp.where(row8 >= shift, pltpu.roll(grp, shift, 0), 0.0)
                if total is not None:
                    grp = grp + total
                total = grp[7:8]
                parts.append(grp)
            cums.append(jnp.concatenate(parts, axis=0))
        yield

        at, lhs, bt, kt = [], [], [], []
        for (c, gi), cum in zip(items, cums):
            rs, ls, slot = chunk_rows(c), lanes(gi), slot_of(c, gi)
            c_last = cum[c_len - 1:c_len]
            kka = kka_ref[rs, ls]
            kx = k_ref[rs, ls]
            g_t = jnp.exp(cum)
            g_last = jnp.exp(c_last)
            g_inv = 1.0 / g_t
            a_t = -kk_ref[rs, ls] * jnp.exp(cum - lw_ref[rs, ls])
            r_t = r_ref[rs, ls] * g_t
            at.append(bf(a_t))
            lhs.append(bf(jnp.concatenate([a_t, r_t], axis=0)))
            b_t = kka * g_inv
            k_t = kx * g_inv
            bt.append(b_t)
            kt.append(k_t)
            bkt_s[slot] = bf(jnp.concatenate([b_t * g_last, k_t * g_last], axis=0).T)
            gm_s[slot] = jnp.broadcast_to(g_last, (VREG_LANES, gl)).T

        sb = [_dot(l, _block_diag_t(b)) for l, b in zip(lhs, bt)]
        sk = [_dot(l, _block_diag_t(k)) for l, k in zip(lhs, kt)]
        for (c, gi), b, k in zip(items, sb, sk):
            arb_s[slot_of(c, gi)] = bf(jnp.where(incl, b[c_len:], 0.0))
        yield

        xs = [jnp.where(strict, b[:c_len], 0.0) for b in sb]
        ts = [eye + x for x in xs]
        xs = [_dot(bf(x), _block_diag(x)) for x in xs]
        yield
        for _ in range(n_sq - 1):
            res = [_dot(bf(jnp.concatenate([x, t], axis=0)), _block_diag(x)) for x, t in zip(xs, ts)]
            xs = [r[:c_len] for r in res]
            ts = [t + r[c_len:] for t, r in zip(ts, res)]
            yield
        ts = [bf(t + _dot(bf(t), _block_diag(x))) for x, t in zip(xs, ts)]
        yield

        kv = [_dot(bf(jnp.concatenate([jnp.where(strict, k[:c_len], 0.0),
                                       jnp.where(incl, k[c_len:], 0.0)], axis=0)),
                   _block_diag(v_ref[chunk_rows(c), lanes(gi)]))
              for (c, gi), k in zip(items, sk)]
        ws = [_dot(t, _block_diag(a)) for t, a in zip(ts, at)]
        u0 = [_dot(t, _block_diag(a[:c_len])) for t, a in zip(ts, kv)]
        for (c, gi), w, u, l, a in zip(items, ws, u0, lhs, kv):
            wr_s[slot_of(c, gi)] = jnp.concatenate([bf(w), l[c_len:]], axis=0)
            u0_s[slot_of(c, gi)] = u
            ykv_s[slot_of(c, gi)] = a[c_len:]

    def advance_chunk(c):
        rs = chunk_rows(c)
        slots = [slot_of(c, gi) for gi in range(n_groups)]
        st = [[state_ref[gi, hr, tl] for hr, tl, _ in head_blocks] for gi in range(n_groups)]
        vs = [v_ref[rs, lanes(gi)] for gi in range(n_groups)]
        ps = [_dot(wr_s[s], state_operand(x)) for s, x in zip(slots, st)]
        yield
        us = [u0_s[s] + p[:c_len] for s, p in zip(slots, ps)]
        upd = [_dot(bkt_s[s], bf(jnp.concatenate([u, v], axis=0))) for s, u, v in zip(slots, us, vs)]
        for gi in range(n_groups):
            gm = gm_s[slots[gi]]
            for piece, (hr, tl, half) in zip(st[gi], head_blocks):
                state_ref[gi, hr, tl] = (piece * gm[hr]
                                         + jnp.where(lane_head == half, upd[gi][hr, tl], 0.0))
        yield
        ys = [p[c_len:] + _dot(arb_s[s], _block_diag(u)) + ykv_s[s] for s, p, u in zip(slots, ps, us)]
        rows_in = []
        for gi, y in enumerate(ys):
            rows_in += [y, r_ref[rs, lanes(gi)] * k_ref[rs, lanes(gi)] * rk_ref[:, lanes(gi)]]
        sums = _dot(bf(jnp.concatenate(rows_in, axis=0)), head_ones)
        yield
        devs = [y - sums[2 * gi * c_len:(2 * gi + 1) * c_len] * inv_n for gi, y in enumerate(ys)]
        var = _dot(bf(jnp.concatenate([dv * dv for dv in devs], axis=0)), head_ones) * inv_n
        for gi in range(n_groups):
            ls = lanes(gi)
            o_ref[rs, ls] = (devs[gi] * lax.rsqrt(var[gi * c_len:(gi + 1) * c_len] + GN_EPS)
                             * lnw_ref[:, ls] + lnb_ref[:, ls]
                             + sums[(2 * gi + 1) * c_len:(2 * gi + 2) * c_len] * vs[gi])

    def run(stages, order):
        for name in order:
            next(stages[name], None)

    assert WKV_PAIR == 2
    run({"p": prepare(0)}, "p" * (n_sq + 4))

    def body(p, carry):
        stages = {"p": prepare(p), "a": advance_chunk((p - 1) * WKV_PAIR),
                  "b": advance_chunk((p - 1) * WKV_PAIR + 1)}
        run(stages, "apapbppbpapapbpbp" + "p" * (n_sq - 5))
        return carry

    for p in range(1, n_pairs):
        body(p, 0)
    last = (n_pairs - 1) * WKV_PAIR
    run({"a": advance_chunk(last), "b": advance_chunk(last + 1)}, "aabababb")

    block = min(tt, WKV_OUT_ROWS)
    for q in range(tt // block):
        rows = slice(q * block, (q + 1) * block)
        z = (o_ref[rows] * g_ref[rows]).astype(BF16)
        o_ref[rows] = x_ref[rows] + gate_ref[0] * _dot(z, wo_ref[...])


def _wkv_mixer(r, k, v, lw, kk, kka, rk, lnw, lnb, g, x, gate, wo, bsz, seq):
    m, d = r.shape
    tt = min(seq, WKV_TT)
    assert tt % (WKV_CHUNK * WKV_PAIR) == 0
    per_seq = seq // tt
    n_groups = d // GROUP_LANES
    n_slots = 2 * WKV_PAIR * n_groups
    tile = pl.BlockSpec((tt, d), lambda b, t: (b * per_seq + t, 0))
    vec = pl.BlockSpec((1, d), lambda b, t: (0, 0))
    return pl.pallas_call(
        _wkv_kernel,
        out_shape=jax.ShapeDtypeStruct((m, d), F32),
        grid=(bsz, per_seq),
        in_specs=[tile] * 6 + [vec] * 3 + [tile, tile,
                  pl.BlockSpec((1, 1, d), lambda b, t: (b, 0, 0)),
                  pl.BlockSpec((d, d), lambda b, t: (0, 0), pipeline_mode=pl.Buffered(1))],
        out_specs=tile,
        scratch_shapes=[pltpu.VMEM((n_groups, GROUP_LANES, GROUP_LANES), F32),
                        pltpu.VMEM((n_slots, 2 * WKV_CHUNK, GROUP_LANES), BF16),
                        pltpu.VMEM((n_slots, WKV_CHUNK, GROUP_LANES), F32),
                        pltpu.VMEM((n_slots, WKV_CHUNK, GROUP_LANES), BF16),
                        pltpu.VMEM((n_slots, WKV_CHUNK, GROUP_LANES), F32),
                        pltpu.VMEM((n_slots, GROUP_LANES, 2 * WKV_CHUNK), BF16),
                        pltpu.VMEM((n_slots, GROUP_LANES, VREG_LANES), F32)],
        compiler_params=_params("parallel", "arbitrary"),
        name="wkv_mixer",
    )(r, k, v, lw, kk, kka, rk, lnw, lnb, g, x, gate, wo)


def kernel(x, c, norm_g, final_g, ada_w, ada_b, conv_w_in, conv_w, conv_w_out, rw_mu, rw_w_rkv, rw_w_o, rw_w0, rw_w1, rw_w2, rw_a0, rw_a1, rw_a2, rw_g1, rw_g2, rw_k_k, rw_k_a, rw_r_k, rw_ln_w, rw_ln_b, rw_v0, rw_v1, rw_v2, mlp_w1, mlp_w2):
    bsz, seq, d = x.shape
    depth = ada_w.shape[0]
    assert d % GROUP_LANES == 0 and seq % WKV_CHUNK == 0
    bf = lambda w: w.astype(BF16)

    mod = _ada_mod(c, ada_w, ada_b).reshape(depth, bsz, 6, 1, d)
    fg = final_g.reshape(1, d)
    xf = x.reshape(bsz * seq, d)
    v_first = None
    for i in range(depth):
        sh1, sc1, gt1, sh2, sc2, gt2 = (mod[i, :, n] for n in range(6))
        a1 = norm_g[i, 0] * (1.0 + sc1)
        a2 = norm_g[i, 1] * (1.0 + sc2)
        j = i // 2
        if i % 2 == 0:
            xf = _conv_mixer(xf, a1, sh1, gt1, bf(conv_w_in[j]), conv_w[j], bf(conv_w_out[j]), seq)
        else:
            vec = jnp.stack([rw_w0[j], rw_a0[j], rw_k_k[j], rw_k_a[j],
                             rw_v0[j - 1] if v_first is not None else jnp.zeros((d,), F32)])
            weights = [bf(rw_w_rkv[j, 0]), bf(rw_w_rkv[j, 1]), bf(rw_w_rkv[j, 2]),
                       bf(rw_w1[j]), bf(rw_w2[j]), bf(rw_a1[j]), bf(rw_a2[j]),
                       bf(rw_g1[j]), bf(rw_g2[j])]
            vres_w = None if v_first is None else [bf(rw_v1[j - 1]), bf(rw_v2[j - 1])]
            r, k, v, lw, kk, kka, g = _rwkv_in(xf, a1, sh1, rw_mu[j], vec, weights,
                                               v_first, vres_w, seq)
            if v_first is None:
                v_first = v
            xf = _wkv_mixer(r, k, v, lw, kk, kka, rw_r_k[j].reshape(1, d), rw_ln_w[j].reshape(1, d),
                            rw_ln_b[j].reshape(1, d), g, xf, gt1, bf(rw_w_o[j]), bsz, seq)
        xf = _mlp(xf, a2, sh2, gt2, fg, mlp_w1, mlp_w2, i, seq, i == depth - 1)
    return xf.reshape(bsz, seq, d)
```

```python
import functools

import jax
import jax.numpy as jnp
from jax import lax
from jax.experimental import pallas as pl
from jax.experimental.pallas import tpu as pltpu

F32 = jnp.float32
BF16 = jnp.bfloat16

HEAD_SIZE = 64
WKV_CHUNK = 64
GROUP_LANES = 256
VREG_LANES = 128
WKV_PAIR = 2
NORM_EPS = 1e-6
GN_EPS = 64e-5
VMEM_LIMIT_BYTES = 56 * 1024 * 1024

MLP_TM = 1024
MLP_TF = 1024
MLP_ROW_BLOCKS = 4
CONV_TM = 1024
CONV_ROW_BLOCKS = 2
RWKV_IN_TM = 512
RWKV_IN_ROW_BLOCKS = 1
WKV_TT = 512
WKV_OUT_ROWS = 512


def _params(*semantics):
    return pltpu.CompilerParams(dimension_semantics=semantics,
                                vmem_limit_bytes=VMEM_LIMIT_BYTES)


def _dot(a, b):
    return jnp.dot(a, b, preferred_element_type=F32)


def _sigmoid(x):
    return 1.0 / (1.0 + jnp.exp2(x * (-1.4426950408889634)))


def _norm_mod(x, a, s):
    ms = jnp.mean(x * x, axis=-1, keepdims=True)
    return x * lax.rsqrt(ms + NORM_EPS) * a + s


def _head_mask():
    shape = (GROUP_LANES, GROUP_LANES)
    return (lax.broadcasted_iota(jnp.int32, shape, 0) // HEAD_SIZE
            == lax.broadcasted_iota(jnp.int32, shape, 1) // HEAD_SIZE)


def _head_ones():
    return jnp.where(_head_mask(), 1.0, 0.0).astype(BF16)


def _shift_rows(x, prev, n):
    out = pltpu.roll(x, n, 0)
    row = lax.broadcasted_iota(jnp.int32, (8, 1), 0)
    head = out[:8]
    for j in range(n):
        head = jnp.where(row == j, prev[8 - n + j:8 - n + j + 1], head)
    return jnp.concatenate([head, out[8:]], axis=0)


def _ada_kernel(c_ref, w_ref, b_ref, o_ref):
    c = c_ref[...]
    ca = (c * _sigmoid(c)).astype(BF16)
    o_ref[0] = _dot(ca, w_ref[0].astype(BF16)) + b_ref[0]


def _ada_mod(c, ada_w, ada_b):
    depth, d, n = ada_w.shape
    bsz = c.shape[0]
    rows = 16
    cp = jnp.zeros((rows, d), F32).at[:bsz].set(c)
    tn = min(n, 2048)
    out = pl.pallas_call(
        _ada_kernel,
        out_shape=jax.ShapeDtypeStruct((depth, rows, n), F32),
        grid=(depth, n // tn),
        in_specs=[pl.BlockSpec((rows, d), lambda i, j: (0, 0)),
                  pl.BlockSpec((1, d, tn), lambda i, j: (i, 0, j)),
                  pl.BlockSpec((1, 1, tn), lambda i, j: (i, 0, j))],
        out_specs=pl.BlockSpec((1, rows, tn), lambda i, j: (i, 0, j)),
        compiler_params=_params("parallel", "parallel"),
        name="ada_mod",
    )(cp, ada_w, ada_b.reshape(depth, 1, n))
    return out[:, :bsz]


def _mlp_kernel(x_ref, a_ref, s_ref, g_ref, fg_ref, w1_ref, w2_ref, o_ref, acc_ref, *, final_norm):
    f = pl.program_id(1)
    n_f = pl.num_programs(1)

    def step(first, last):
        w1 = w1_ref[0].astype(BF16)
        w2 = w2_ref[0].astype(BF16)
        rows_per_block = x_ref.shape[0] // MLP_ROW_BLOCKS
        for q in range(MLP_ROW_BLOCKS):
            rows = slice(q * rows_per_block, (q + 1) * rows_per_block)
            x = x_ref[rows]
            h = _norm_mod(x, a_ref[0], s_ref[0]).astype(BF16)
            z = jnp.maximum(_dot(h, w1), 0.0)
            part = _dot((z * z).astype(BF16), w2)
            acc = part if first else acc_ref[rows] + part
            if not last:
                acc_ref[rows] = acc
                continue
            y = x + g_ref[0] * acc
            if final_norm:
                ms = jnp.mean(y * y, axis=-1, keepdims=True)
                y = y * lax.rsqrt(ms + NORM_EPS) * fg_ref[...]
            o_ref[rows] = y

    pl.when(f == 0)(functools.partial(step, True, False))
    pl.when(jnp.logical_and(f > 0, f < n_f - 1))(functools.partial(step, False, False))
    pl.when(f == n_f - 1)(functools.partial(step, False, True))


def _mlp(x, a, s, g, fg, w1, w2, layer, seq, final_norm):
    m, d = x.shape
    ff = w1.shape[2]
    tm = min(seq, MLP_TM)
    tf = min(ff, MLP_TF)
    assert ff // tf >= 2
    per_seq = seq // tm
    vec = pl.BlockSpec((1, 1, d), lambda i, f: (i // per_seq, 0, 0))
    return pl.pallas_call(
        functools.partial(_mlp_kernel, final_norm=final_norm),
        out_shape=jax.ShapeDtypeStruct((m, d), F32),
        grid=(m // tm, ff // tf),
        in_specs=[pl.BlockSpec((tm, d), lambda i, f: (i, 0)), vec, vec, vec,
                  pl.BlockSpec((1, d), lambda i, f: (0, 0)),
                  pl.BlockSpec((1, d, tf), lambda i, f: (layer, 0, f)),
                  pl.BlockSpec((1, tf, d), lambda i, f: (layer, f, 0))],
        out_specs=pl.BlockSpec((tm, d), lambda i, f: (i, 0)),
        scratch_shapes=[pltpu.VMEM((tm, d), F32)],
        compiler_params=_params("parallel", "arbitrary"),
        name="mlp",
    )(x, a, s, g, fg, w1, w2)


def _conv_kernel(x_ref, a_ref, s_ref, g_ref, win_ref, cw_ref, wo_ref, o_ref, carry_ref, *, per_seq):
    i = pl.program_id(0)
    tm, d = x_ref.shape

    @pl.when(i % per_seq == 0)
    def _():
        carry_ref[...] = jnp.zeros_like(carry_ref)

    prev = carry_ref[...]
    cw = cw_ref[...]
    block = tm // CONV_ROW_BLOCKS
    for q in range(CONV_ROW_BLOCKS):
        rows = slice(q * block, (q + 1) * block)
        x = x_ref[rows]
        h = _norm_mod(x, a_ref[0], s_ref[0]).astype(BF16)
        u = _dot(h, win_ref[:, d:2 * d]) * _dot(h, win_ref[:, 2 * d:])
        conv = cw[0:1] * _shift_rows(u, prev, 2) + cw[1:2] * _shift_rows(u, prev, 1) + cw[2:3] * u
        prev = u[block - 8:block]
        y = (_dot(h, win_ref[:, :d]) * conv).astype(BF16)
        o_ref[rows] = x + g_ref[0] * _dot(y, wo_ref[...])
    carry_ref[...] = prev


def _conv_mixer(x, a, s, g, w_in, cw, wo, seq):
    m, d = x.shape
    tm = min(seq, CONV_TM)
    per_seq = seq // tm
    tile = pl.BlockSpec((tm, d), lambda i: (i, 0))
    vec = pl.BlockSpec((1, 1, d), lambda i: (i // per_seq, 0, 0))
    return pl.pallas_call(
        functools.partial(_conv_kernel, per_seq=per_seq),
        out_shape=jax.ShapeDtypeStruct((m, d), F32),
        grid=(m // tm,),
        in_specs=[tile, vec, vec, vec,
                  pl.BlockSpec(w_in.shape, lambda i: (0, 0), pipeline_mode=pl.Buffered(1)),
                  pl.BlockSpec(cw.shape, lambda i: (0, 0)),
                  pl.BlockSpec((d, d), lambda i: (0, 0), pipeline_mode=pl.Buffered(1))],
        out_specs=tile,
        scratch_shapes=[pltpu.VMEM((8, d), F32)],
        compiler_params=_params("arbitrary"),
        name="conv_mixer",
    )(x, a, s, g, w_in, cw, wo)


def _rwkv_in_kernel(*refs, per_seq, use_vres):
    if use_vres:
        (x_ref, a_ref, s_ref, mu_ref, vec_ref, wr_ref, wk_ref, wv_ref, w1_ref, w2_ref,
         a1_ref, a2_ref, g1_ref, g2_ref, vf_ref, v1_ref, v2_ref,
         r_ref, k_ref, v_ref, lw_ref, kk_ref, kka_ref, g_ref, carry_ref) = refs
    else:
        (x_ref, a_ref, s_ref, mu_ref, vec_ref, wr_ref, wk_ref, wv_ref, w1_ref, w2_ref,
         a1_ref, a2_ref, g1_ref, g2_ref,
         r_ref, k_ref, v_ref, lw_ref, kk_ref, kka_ref, g_ref, carry_ref) = refs
    i = pl.program_id(0)
    tm, d = x_ref.shape

    @pl.when(i % per_seq == 0)
    def _():
        carry_ref[...] = jnp.zeros_like(carry_ref)

    mu = mu_ref[...]
    vec = vec_ref[...]
    ones = _head_ones()
    prev = carry_ref[...]

    block = tm // RWKV_IN_ROW_BLOCKS
    for q in range(RWKV_IN_ROW_BLOCKS):
        rows = slice(q * block, (q + 1) * block)
        h = _norm_mod(x_ref[rows], a_ref[0], s_ref[0])
        xx = _shift_rows(h, prev, 1) - h
        prev = h[block - 8:block]

        def mix(n):
            return (h + xx * mu[n:n + 1]).astype(BF16)

        r_ref[rows] = _dot(mix(0), wr_ref[...])

        wl = vec[0:1] + _dot(jnp.tanh(_dot(mix(1), w1_ref[...])).astype(BF16), w2_ref[...])
        lw_ref[rows] = -jnp.exp(-0.5) * _sigmoid(wl)

        g_ref[rows] = _dot(_sigmoid(_dot(mix(5), g1_ref[...])).astype(BF16), g2_ref[...])

        xv = mix(3)
        v = _dot(xv, wv_ref[...])
        if use_vres:
            mix_v = _sigmoid(vec[4:5] + _dot(_dot(xv, v1_ref[...]).astype(BF16), v2_ref[...]))
            v = v + (vf_ref[rows] - v) * mix_v
        v_ref[rows] = v

        a = _sigmoid(vec[1:2] + _dot(_dot(mix(4), a1_ref[...]).astype(BF16), a2_ref[...]))
        k = _dot(mix(2), wk_ref[...])
        k_ref[rows] = k * ((1.0 - vec[3:4]) + a * vec[3:4])
        kk = k * vec[2:3]
        sq = (kk * kk).astype(BF16)
        ss = jnp.concatenate([_dot(sq[:, n * GROUP_LANES:(n + 1) * GROUP_LANES], ones)
                              for n in range(d // GROUP_LANES)], axis=1)
        kk = kk * jnp.minimum(lax.rsqrt(ss), 1e12)
        kk_ref[rows] = kk
        kka_ref[rows] = kk * a
    carry_ref[...] = prev


def _rwkv_in(x, a, s, mu, vec, weights, vfirst, vres_w, seq):
    m, d = x.shape
    tm = min(seq, RWKV_IN_TM)
    per_seq = seq // tm
    use_vres = vfirst is not None
    tile = pl.BlockSpec((tm, d), lambda i: (i, 0))
    mvec = pl.BlockSpec((1, 1, d), lambda i: (i // per_seq, 0, 0))

    def full(arr):
        return pl.BlockSpec(arr.shape, lambda i: (0,) * arr.ndim, pipeline_mode=pl.Buffered(1))

    args = [x, a, s, mu, vec] + list(weights)
    specs = [tile, mvec, mvec, full(mu), full(vec)] + [full(w) for w in weights]
    if use_vres:
        args += [vfirst] + list(vres_w)
        specs += [tile] + [full(w) for w in vres_w]
    out = jax.ShapeDtypeStruct((m, d), F32)
    return pl.pallas_call(
        functools.partial(_rwkv_in_kernel, per_seq=per_seq, use_vres=use_vres),
        out_shape=(out,) * 7,
        grid=(m // tm,),
        in_specs=specs,
        out_specs=(tile,) * 7,
        scratch_shapes=[pltpu.VMEM((8, d), F32)],
        compiler_params=_params("arbitrary"),
        name="rwkv_in",
    )(*args)


def _block_diag(x):
    n_rows, width = x.shape
    xb = x.astype(BF16)
    lane_head = lax.broadcasted_iota(jnp.int32, (n_rows, VREG_LANES), 1) // HEAD_SIZE
    zero = jnp.zeros((n_rows, VREG_LANES), BF16)
    n_tiles = width // VREG_LANES
    blocks = []
    for h in range(width // HEAD_SIZE):
        tile, sub = divmod(h * HEAD_SIZE, VREG_LANES)
        piece = jnp.where(lane_head == sub // HEAD_SIZE,
                          xb[:, tile * VREG_LANES:(tile + 1) * VREG_LANES], zero)
        blocks.append(jnp.concatenate([piece if n == tile else zero for n in range(n_tiles)], axis=1))
    return jnp.concatenate(blocks, axis=0)


def _block_diag_t(x):
    n_rows, width = x.shape
    reps = VREG_LANES // n_rows
    shape = (VREG_LANES, VREG_LANES)
    keep = (lax.broadcasted_iota(jnp.int32, shape, 0) // HEAD_SIZE
            == lax.broadcasted_iota(jnp.int32, shape, 1) // n_rows)
    n_tiles = width // VREG_LANES
    zero = jnp.zeros(shape, BF16)
    blocks = []
    for tile in range(n_tiles):
        xt = x[:, tile * VREG_LANES:(tile + 1) * VREG_LANES]
        sq = jnp.where(keep, jnp.concatenate([xt] * reps, axis=0).T.astype(BF16), zero)
        blocks.append(jnp.concatenate([sq if n == tile else zero for n in range(n_tiles)], axis=1))
    return jnp.concatenate(blocks, axis=0)


def _wkv_kernel(r_ref, k_ref, v_ref, lw_ref, kk_ref, kka_ref, rk_ref, lnw_ref, lnb_ref,
                g_ref, x_ref, gate_ref, wo_ref,
                o_ref, state_ref, wr_s, u0_s, arb_s, ykv_s, bkt_s, gm_s):
    tt, d = r_ref.shape
    c_len, gl = WKV_CHUNK, GROUP_LANES
    n_pairs = tt // (c_len * WKV_PAIR)
    n_groups = d // gl
    n_sq = (c_len - 1).bit_length() - 1

    @pl.when(pl.program_id(1) == 0)
    def _():
        state_ref[...] = jnp.zeros_like(state_ref)

    row = lax.broadcasted_iota(jnp.int32, (c_len, gl), 0)
    col = lax.broadcasted_iota(jnp.int32, (c_len, gl), 1) % c_len
    row8 = lax.broadcasted_iota(jnp.int32, (8, gl), 0)
    lane_head = lax.broadcasted_iota(jnp.int32, (HEAD_SIZE, VREG_LANES), 1) // HEAD_SIZE
    strict = col < row
    incl = col <= row
    eye = jnp.where(col == row, 1.0, 0.0)
    bd_mask = _head_mask()
    head_ones = jnp.where(bd_mask, 1.0, 0.0).astype(BF16)
    inv_n = 1.0 / HEAD_SIZE

    def bf(x):
        return x.astype(BF16)

    head_blocks = []
    for h in range(gl // HEAD_SIZE):
        tile, sub = divmod(h * HEAD_SIZE, VREG_LANES)
        head_blocks.append((slice(h * HEAD_SIZE, (h + 1) * HEAD_SIZE),
                            slice(tile * VREG_LANES, (tile + 1) * VREG_LANES), sub // HEAD_SIZE))

    def state_operand(pieces):
        zero = jnp.zeros((HEAD_SIZE, VREG_LANES), BF16)
        n_tiles = gl // VREG_LANES
        return jnp.concatenate(
            [jnp.concatenate([bf(p) if n * VREG_LANES == tl.start else zero for n in range(n_tiles)], axis=1)
             for p, (_, tl, _) in zip(pieces, head_blocks)], axis=0)

    def lanes(gi):
        return slice(gi * gl, (gi + 1) * gl)

    def slot_of(c, gi):
        ring = 2 * WKV_PAIR
        return (c % ring if isinstance(c, int) else lax.rem(c, ring)) * n_groups + gi

    def chunk_rows(c):
        start = c * c_len
        return pl.ds(start if isinstance(start, int) else pl.multiple_of(start, c_len), c_len)

    def prepare(p):
        items = [(p * WKV_PAIR + ci, gi) for ci in range(WKV_PAIR) for gi in range(n_groups)]

        cums = []
        for c, gi in items:
            lw = lw_ref[chunk_rows(c), lanes(gi)]
            parts, total = [], None
            for j in range(c_len // 8):
                grp = lw[8 * j:8 * j + 8]
                for shift in (1, 2, 4):
                    grp = grp + jnp.where(row8 >= shift, pltpu.roll(grp, shift, 0), 0.0)
                if total is not None:
                    grp = grp + total
                total = grp[7:8]
                parts.append(grp)
            cums.append(jnp.concatenate(parts, axis=0))
        yield

        at, lhs, bt, kt = [], [], [], []
        for (c, gi), cum in zip(items, cums):
            rs, ls, slot = chunk_rows(c), lanes(gi), slot_of(c, gi)
            c_last = cum[c_len - 1:c_len]
            kka = kka_ref[rs, ls]
            kx = k_ref[rs, ls]
            g_t = jnp.exp(cum)
            g_last = jnp.exp(c_last)
            g_inv = 1.0 / g_t
            a_t = -kk_ref[rs, ls] * jnp.exp(cum - lw_ref[rs, ls])
            r_t = r_ref[rs, ls] * g_t
            at.append(bf(a_t))
            lhs.append(bf(jnp.concatenate([a_t, r_t], axis=0)))
            b_t = kka * g_inv
            k_t = kx * g_inv
            bt.append(b_t)
            kt.append(k_t)
            bkt_s[slot] = bf(jnp.concatenate([b_t * g_last, k_t * g_last], axis=0).T)
            gm_s[slot] = jnp.broadcast_to(g_last, (VREG_LANES, gl)).T

        sb = [_dot(l, _block_diag_t(b)) for l, b in zip(lhs, bt)]
        sk = [_dot(l, _block_diag_t(k)) for l, k in zip(lhs, kt)]
        for (c, gi), b, k in zip(items, sb, sk):
            arb_s[slot_of(c, gi)] = bf(jnp.where(incl, b[c_len:], 0.0))
        yield

        xs = [jnp.where(strict, b[:c_len], 0.0) for b in sb]
        ts = [eye + x for x in xs]
        xs = [_dot(bf(x), _block_diag(x)) for x in xs]
        yield
        for _ in range(n_sq - 1):
            res = [_dot(bf(jnp.concatenate([x, t], axis=0)), _block_diag(x)) for x, t in zip(xs, ts)]
            xs = [r[:c_len] for r in res]
            ts = [t + r[c_len:] for t, r in zip(ts, res)]
            yield
        ts = [bf(t + _dot(bf(t), _block_diag(x))) for x, t in zip(xs, ts)]
        yield

        kv = [_dot(bf(jnp.concatenate([jnp.where(strict, k[:c_len], 0.0),
                                       jnp.where(incl, k[c_len:], 0.0)], axis=0)),
                   _block_diag(v_ref[chunk_rows(c), lanes(gi)]))
              for (c, gi), k in zip(items, sk)]
        ws = [_dot(t, _block_diag(a)) for t, a in zip(ts, at)]
        u0 = [_dot(t, _block_diag(a[:c_len])) for t, a in zip(ts, kv)]
        for (c, gi), w, u, l, a in zip(items, ws, u0, lhs, kv):
            wr_s[slot_of(c, gi)] = jnp.concatenate([bf(w), l[c_len:]], axis=0)
            u0_s[slot_of(c, gi)] = u
            ykv_s[slot_of(c, gi)] = a[c_len:]

    def advance_chunk(c):
        rs = chunk_rows(c)
        slots = [slot_of(c, gi) for gi in range(n_groups)]
        st = [[state_ref[gi, hr, tl] for hr, tl, _ in head_blocks] for gi in range(n_groups)]
        vs = [v_ref[rs, lanes(gi)] for gi in range(n_groups)]
        ps = [_dot(wr_s[s], state_operand(x)) for s, x in zip(slots, st)]
        yield
        us = [u0_s[s] + p[:c_len] for s, p in zip(slots, ps)]
        upd = [_dot(bkt_s[s], bf(jnp.concatenate([u, v], axis=0))) for s, u, v in zip(slots, us, vs)]
        for gi in range(n_groups):
            gm = gm_s[slots[gi]]
            for piece, (hr, tl, half) in zip(st[gi], head_blocks):
                state_ref[gi, hr, tl] = (piece * gm[hr]
                                         + jnp.where(lane_head == half, upd[gi][hr, tl], 0.0))
        yield
        ys = [p[c_len:] + _dot(arb_s[s], _block_diag(u)) + ykv_s[s] for s, p, u in zip(slots, ps, us)]
        rows_in = []
        for gi, y in enumerate(ys):
            rows_in += [y, r_ref[rs, lanes(gi)] * k_ref[rs, lanes(gi)] * rk_ref[:, lanes(gi)]]
        sums = _dot(bf(jnp.concatenate(rows_in, axis=0)), head_ones)
        yield
        devs = [y - sums[2 * gi * c_len:(2 * gi + 1) * c_len] * inv_n for gi, y in enumerate(ys)]
        var = _dot(bf(jnp.concatenate([dv * dv for dv in devs], axis=0)), head_ones) * inv_n
        for gi in range(n_groups):
            ls = lanes(gi)
            o_ref[rs, ls] = (devs[gi] * lax.rsqrt(var[gi * c_len:(gi + 1) * c_len] + GN_EPS)
                             * lnw_ref[:, ls] + lnb_ref[:, ls]
                             + sums[(2 * gi + 1) * c_len:(2 * gi + 2) * c_len] * vs[gi])

    def run(stages, order):
        for name in order:
            next(stages[name], None)

    assert WKV_PAIR == 2
    run({"p": prepare(0)}, "p" * (n_sq + 4))

    def body(p, carry):
        stages = {"p": prepare(p), "a": advance_chunk((p - 1) * WKV_PAIR),
                  "b": advance_chunk((p - 1) * WKV_PAIR + 1)}
        run(stages, "appappbppbpapabpb" + "p" * (n_sq - 5))
        return carry

    for p in range(1, n_pairs):
        body(p, 0)
    last = (n_pairs - 1) * WKV_PAIR
    run({"a": advance_chunk(last), "b": advance_chunk(last + 1)}, "aabababb")

    block = min(tt, WKV_OUT_ROWS)
    for q in range(tt // block):
        rows = slice(q * block, (q + 1) * block)
        z = (o_ref[rows] * g_ref[rows]).astype(BF16)
        o_ref[rows] = x_ref[rows] + gate_ref[0] * _dot(z, wo_ref[...])


def _wkv_mixer(r, k, v, lw, kk, kka, rk, lnw, lnb, g, x, gate, wo, bsz, seq):
    m, d = r.shape
    tt = min(seq, WKV_TT)
    assert tt % (WKV_CHUNK * WKV_PAIR) == 0
    per_seq = seq // tt
    n_groups = d // GROUP_LANES
    n_slots = 2 * WKV_PAIR * n_groups
    tile = pl.BlockSpec((tt, d), lambda b, t: (b * per_seq + t, 0))
    vec = pl.BlockSpec((1, d), lambda b, t: (0, 0))
    return pl.pallas_call(
        _wkv_kernel,
        out_shape=jax.ShapeDtypeStruct((m, d), F32),
        grid=(bsz, per_seq),
        in_specs=[tile] * 6 + [vec] * 3 + [tile, tile,
                  pl.BlockSpec((1, 1, d), lambda b, t: (b, 0, 0)),
                  pl.BlockSpec((d, d), lambda b, t: (0, 0), pipeline_mode=pl.Buffered(1))],
        out_specs=tile,
        scratch_shapes=[pltpu.VMEM((n_groups, GROUP_LANES, GROUP_LANES), F32),
                        pltpu.VMEM((n_slots, 2 * WKV_CHUNK, GROUP_LANES), BF16),
                        pltpu.VMEM((n_slots, WKV_CHUNK, GROUP_LANES), F32),
                        pltpu.VMEM((n_slots, WKV_CHUNK, GROUP_LANES), BF16),
                        pltpu.VMEM((n_slots, WKV_CHUNK, GROUP_LANES), F32),
                        pltpu.VMEM((n_slots, GROUP_LANES, 2 * WKV_CHUNK), BF16),
                        pltpu.VMEM((n_slots, GROUP_LANES, VREG_LANES), F32)],
        compiler_params=_params("parallel", "arbitrary"),
        name="wkv_mixer",
    )(r, k, v, lw, kk, kka, rk, lnw, lnb, g, x, gate, wo)


def kernel(x, c, norm_g, final_g, ada_w, ada_b, conv_w_in, conv_w, conv_w_out, rw_mu, rw_w_rkv, rw_w_o, rw_w0, rw_w1, rw_w2, rw_a0, rw_a1, rw_a2, rw_g1, rw_g2, rw_k_k, rw_k_a, rw_r_k, rw_ln_w, rw_ln_b, rw_v0, rw_v1, rw_v2, mlp_w1, mlp_w2):
    bsz, seq, d = x.shape
    depth = ada_w.shape[0]
    assert d % GROUP_LANES == 0 and seq % WKV_CHUNK == 0
    bf = lambda w: w.astype(BF16)

    mod = _ada_mod(c, ada_w, ada_b).reshape(depth, bsz, 6, 1, d)
    fg = final_g.reshape(1, d)
    xf = x.reshape(bsz * seq, d)
    v_first = None
    for i in range(depth):
        sh1, sc1, gt1, sh2, sc2, gt2 = (mod[i, :, n] for n in range(6))
        a1 = norm_g[i, 0] * (1.0 + sc1)
        a2 = norm_g[i, 1] * (1.0 + sc2)
        j = i // 2
        if i % 2 == 0:
            xf = _conv_mixer(xf, a1, sh1, gt1, bf(conv_w_in[j]), conv_w[j], bf(conv_w_out[j]), seq)
        else:
            vec = jnp.stack([rw_w0[j], rw_a0[j], rw_k_k[j], rw_k_a[j],
                             rw_v0[j - 1] if v_first is not None else jnp.zeros((d,), F32)])
            weights = [bf(rw_w_rkv[j, 0]), bf(rw_w_rkv[j, 1]), bf(rw_w_rkv[j, 2]),
                       bf(rw_w1[j]), bf(rw_w2[j]), bf(rw_a1[j]), bf(rw_a2[j]),
                       bf(rw_g1[j]), bf(rw_g2[j])]
            vres_w = None if v_first is None else [bf(rw_v1[j - 1]), bf(rw_v2[j - 1])]
            r, k, v, lw, kk, kka, g = _rwkv_in(xf, a1, sh1, rw_mu[j], vec, weights,
                                               v_first, vres_w, seq)
            if v_first is None:
                v_first = v
            xf = _wkv_mixer(r, k, v, lw, kk, kka, rw_r_k[j].reshape(1, d), rw_ln_w[j].reshape(1, d),
                            rw_ln_b[j].reshape(1, d), g, xf, gt1, bf(rw_w_o[j]), bsz, seq)
        xf = _mlp(xf, a2, sh2, gt2, fg, mlp_w1, mlp_w2, i, seq, i == depth - 1)
    return xf.reshape(bsz, seq, d)
```
